```python
import math
import jax, jax.numpy as jnp
from jax import lax
import numpy as np

D_MODEL = 1024
BATCH = 8
SEQ = 2048
DEPTH = 1
DEC_BATCH = 128
DEC_SEQ = 1
PAST_LEN = 16384
PAGE_SIZE = 128

N_META = 16
D_MIX = 2 * D_MODEL
D_SSM = D_MIX // 2
D_CONF = D_MIX - D_SSM
SSD_HEAD_DIM = 64
SSD_HEADS = D_SSM // SSD_HEAD_DIM
SSD_GROUPS = 2
SSD_STATE = 128
SSD_CONV = 4
SSD_CONV_DIM = D_SSM + 2 * SSD_GROUPS * SSD_STATE
CHUNK = 128
CONF_WIDTH = 31
N_EXPERTS = 32
TOP_K = 4
D_FF = D_MODEL
SWIGLU_LIMIT = 7.0
SWIGLU_ALPHA = 1.702
MOE_BLOCK = 128
EPS = 1e-5
D_IN_PROJ = D_SSM + SSD_CONV_DIM + SSD_HEADS + 2 * D_CONF

kernel_name = 'hymba_ssd_conformer_moe_step'

F32 = jnp.float32


def rmsnorm(x, g):
    xf = x.astype(F32)
    r = xf * lax.rsqrt(jnp.mean(xf * xf, axis=-1, keepdims=True) + EPS)
    return (r * g.astype(F32)).astype(x.dtype)


def layernorm(x, g, b):
    xf = x.astype(F32)
    mu = jnp.mean(xf, axis=-1, keepdims=True)
    var = jnp.mean(jnp.square(xf - mu), axis=-1, keepdims=True)
    return ((xf - mu) * lax.rsqrt(var + EPS) * g.astype(F32) + b.astype(F32)).astype(x.dtype)


def causal_dwconv(seq, hist, w, b):
    full = jnp.concatenate([hist.astype(seq.dtype), seq], axis=1)
    k = w.shape[0]
    out = lax.conv_general_dilated(full, w[:, None, :].astype(seq.dtype), (1,), 'VALID',
                                   dimension_numbers=('NWC', 'WIO', 'NWC'),
                                   feature_group_count=seq.shape[-1])
    return out + b.astype(seq.dtype), full[:, full.shape[1] - (k - 1):]


def ssd_chunked(x, dt, a, bm, cm, h0, front_pad):
    b, length, n_heads, p = x.shape
    g, n = bm.shape[2], bm.shape[3]
    r = n_heads // g
    total = length + front_pad
    q = min(CHUNK, total)
    end_pad = (-total) % q
    nc = (total + end_pad) // q

    def pad(t):
        return jnp.pad(t.astype(F32), [(0, 0), (front_pad, end_pad)] + [(0, 0)] * (t.ndim - 2))

    x = pad(x).reshape(b, nc, q, g, r, p)
    dt = pad(dt).reshape(b, nc, q, g, r)
    bm = pad(bm).reshape(b, nc, q, g, n)
    cm = pad(cm).reshape(b, nc, q, g, n)
    a_cum = jnp.cumsum(dt * a.astype(F32).reshape(g, r), axis=2)
    xdt = x * dt[..., None]
    causal = jnp.tril(jnp.ones((q, q), dtype=bool))[:, :, None, None]
    seg = a_cum[:, :, :, None] - a_cum[:, :, None, :]
    decay_ls = jnp.exp(jnp.where(causal, seg, -jnp.inf))
    cb = jnp.einsum('bclgn,bcsgn->bclsg', cm, bm)
    y_diag = jnp.einsum('bclsg,bclsgr,bcsgrp->bclgrp', cb, decay_ls, xdt)
    decay_to_end = jnp.exp(a_cum[:, :, -1:] - a_cum)
    chunk_states = jnp.einsum('bcsgn,bcsgr,bcsgrp->bcgrpn', bm, decay_to_end, xdt)
    chunk_decay = jnp.exp(a_cum[:, :, -1])

    def step(h, inp):
        s, d = inp
        return d[..., None, None] * h + s, h

    h_final, h_enter = lax.scan(step, h0.astype(F32).reshape(b, g, r, p, n),
                                (jnp.moveaxis(chunk_states, 1, 0), jnp.moveaxis(chunk_decay, 1, 0)))
    h_enter = jnp.moveaxis(h_enter, 0, 1)
    y_off = jnp.einsum('bclgn,bcgrpn,bclgr->bclgrp', cm, h_enter, jnp.exp(a_cum))
    y = (y_diag + y_off).reshape(b, nc * q, n_heads, p)[:, front_pad:front_pad + length]
    return y, h_final.reshape(b, n_heads, p, n)


def mixer_groups(u, ssm0, ssd_hist, conf_hist, front_pad, w_in, conv_ssd_w, conv_ssd_b, dt_bias,
                 a_log, d_skip, g_ssd_norm, conv_conf_w, conv_conf_b, ln_conf_g, ln_conf_b, w_out):
    b, length, _ = u.shape
    proj = u @ w_in
    z, xbc, dt_raw, glu_in = jnp.split(
        proj, [D_SSM, D_SSM + SSD_CONV_DIM, D_SSM + SSD_CONV_DIM + SSD_HEADS], axis=-1)
    xbc, ssd_hist_new = causal_dwconv(xbc, ssd_hist, conv_ssd_w, conv_ssd_b)
    xbc = jax.nn.silu(xbc)
    xs, bm, cm = jnp.split(xbc, [D_SSM, D_SSM + SSD_GROUPS * SSD_STATE], axis=-1)
    xs = xs.reshape(b, length, SSD_HEADS, SSD_HEAD_DIM)
    bm = bm.reshape(b, length, SSD_GROUPS, SSD_STATE)
    cm = cm.reshape(b, length, SSD_GROUPS, SSD_STATE)
    dt = jax.nn.softplus(dt_raw.astype(F32) + dt_bias.astype(F32))
    a = -jnp.exp(a_log.astype(F32))
    y, ssm_new = ssd_chunked(xs, dt, a, bm, cm, ssm0, front_pad)
    y = y + xs.astype(F32) * d_skip.astype(F32)[:, None]
    grp = (b, length, SSD_GROUPS, D_SSM // SSD_GROUPS)
    y = y.reshape(grp) * jax.nn.silu(z.astype(F32)).reshape(grp)
    y = rmsnorm(y, g_ssd_norm.reshape(SSD_GROUPS, -1)).reshape(b, length, D_SSM).astype(u.dtype)
    val, gate = jnp.split(glu_in, 2, axis=-1)
    c = val * jax.nn.sigmoid(gate)
    c, conf_hist_new = causal_dwconv(c, conf_hist, conv_conf_w, conv_conf_b)
    c = jax.nn.silu(layernorm(c, ln_conf_g, ln_conf_b))
    out = jnp.concatenate([y, c], axis=-1) @ w_out
    return out, ssm_new.astype(ssm0.dtype), ssd_hist_new, conf_hist_new


def moe_ffn(u, w_router, b_router, w_gate, b_gate, w_up, b_up, w_down, b_down):
    lead = u.shape[:-1]
    xf = u.reshape(-1, D_MODEL)
    t = xf.shape[0]
    logits = (xf @ w_router + b_router).astype(F32)
    top_val, top_idx = lax.top_k(logits, TOP_K)
    gates = jax.nn.softmax(top_val, axis=-1).astype(u.dtype)
    n_assign = t * TOP_K
    e_flat = top_idx.reshape(-1)
    tok_flat = jnp.arange(n_assign, dtype=jnp.int32) // TOP_K
    g_flat = gates.reshape(-1)
    counts = jnp.bincount(e_flat, length=N_EXPERTS)
    padded = (counts + MOE_BLOCK - 1) // MOE_BLOCK * MOE_BLOCK
    pad_end = jnp.cumsum(padded)
    pad_start = pad_end - padded
    raw_start = jnp.cumsum(counts) - counts
    order = jnp.argsort(e_flat)
    e_sorted = e_flat[order]
    dest = pad_start[e_sorted] + jnp.arange(n_assign) - raw_start[e_sorted]
    n_blocks = -(-n_assign // MOE_BLOCK) + N_EXPERTS
    rows = n_blocks * MOE_BLOCK
    buf_tok = jnp.full((rows,), t, dtype=jnp.int32).at[dest].set(tok_flat[order])
    buf_gate = jnp.zeros((rows,), u.dtype).at[dest].set(g_flat[order])
    block_expert = jnp.minimum(
        jnp.searchsorted(pad_end, jnp.arange(n_blocks) * MOE_BLOCK, side='right'), N_EXPERTS - 1)
    xb = jnp.concatenate([xf, jnp.zeros((1, D_MODEL), xf.dtype)], axis=0)[buf_tok]
    xb = xb.reshape(n_blocks, MOE_BLOCK, D_MODEL)

    def expert_rows(args):
        xblk, e = args
        gt = jnp.minimum(xblk @ w_gate[e] + b_gate[e], SWIGLU_LIMIT)
        up = jnp.clip(xblk @ w_up[e] + b_up[e], -SWIGLU_LIMIT, SWIGLU_LIMIT)
        hdn = (up + 1.0) * gt * jax.nn.sigmoid(gt * SWIGLU_ALPHA)
        return hdn @ w_down[e] + b_down[e]

    yb = lax.map(expert_rows, (xb, block_expert)).reshape(rows, D_MODEL)
    out = jnp.zeros((t + 1, D_MODEL), u.dtype).at[buf_tok].add(yb * buf_gate[:, None])[:t]
    return out.reshape(*lead, D_MODEL)


def decoder_layer(h, ssm0, ssd_hist, conf_hist, front_pad, g_mix, g_ffn, mix_w, moe_w):
    mixed, ssm_new, ssd_new, conf_new = mixer_groups(rmsnorm(h, g_mix), ssm0, ssd_hist, conf_hist,
                                                     front_pad, *mix_w)
    h = h + mixed
    h = h + moe_ffn(rmsnorm(h, g_ffn), *moe_w)
    return h, ssm_new, ssd_new, conf_new


def setup_inputs(seed: int = 0) -> dict:
    key = jax.random.key(seed)
    ks = jax.random.split(key, 32)
    nrm = jax.random.normal
    dt0 = jnp.exp(jax.random.uniform(ks[10], (DEPTH, SSD_HEADS), minval=math.log(1e-3), maxval=math.log(1e-1)))
    return {
        'x_prompt': nrm(ks[0], (BATCH, SEQ, D_MODEL), F32),
        'x_sample': nrm(ks[1], (DEC_BATCH, DEC_SEQ, D_MODEL), F32),
        'state_ssm': 0.5 * nrm(ks[2], (DEPTH, DEC_BATCH, SSD_HEADS, SSD_HEAD_DIM, SSD_STATE), F32),
        'state_ssd_conv': nrm(ks[3], (DEPTH, DEC_BATCH, SSD_CONV - 1, SSD_CONV_DIM), F32),
        'state_conf_conv': 0.5 * nrm(ks[4], (DEPTH, DEC_BATCH, CONF_WIDTH - 1, D_CONF), F32),
        'meta_tokens': nrm(ks[5], (N_META, D_MODEL), F32),
        'g_mix': 1.0 + 0.02 * nrm(ks[6], (DEPTH, D_MODEL), F32),
        'w_in': nrm(ks[7], (DEPTH, D_MODEL, D_IN_PROJ), F32) * D_MODEL ** -0.5,
        'conv_ssd_w': nrm(ks[8], (DEPTH, SSD_CONV, SSD_CONV_DIM), F32) * SSD_CONV ** -0.5,
        'conv_ssd_b': 0.02 * nrm(ks[9], (DEPTH, SSD_CONV_DIM), F32),
        'dt_bias': dt0 + jnp.log(-jnp.expm1(-dt0)),
        'a_log': jnp.log(jax.random.uniform(ks[11], (DEPTH, SSD_HEADS), minval=1.0, maxval=16.0)),
        'd_skip': 1.0 + 0.1 * nrm(ks[12], (DEPTH, SSD_HEADS), F32),
        'g_ssd_norm': 1.0 + 0.02 * nrm(ks[13], (DEPTH, D_SSM), F32),
        'conv_conf_w': nrm(ks[14], (DEPTH, CONF_WIDTH, D_CONF), F32) * CONF_WIDTH ** -0.5,
        'conv_conf_b': 0.02 * nrm(ks[15], (DEPTH, D_CONF), F32),
        'ln_conf_g': 1.0 + 0.02 * nrm(ks[16], (DEPTH, D_CONF), F32),
        'ln_conf_b': 0.02 * nrm(ks[17], (DEPTH, D_CONF), F32),
        'w_out': nrm(ks[18], (DEPTH, D_MIX, D_MODEL), F32) * D_MIX ** -0.5,
        'g_ffn': 1.0 + 0.02 * nrm(ks[19], (DEPTH, D_MODEL), F32),
        'w_router': nrm(ks[20], (DEPTH, D_MODEL, N_EXPERTS), F32) * D_MODEL ** -0.5,
        'b_router': 0.01 * nrm(ks[21], (DEPTH, N_EXPERTS), F32),
        'w_gate': nrm(ks[22], (DEPTH, N_EXPERTS, D_MODEL, D_FF), F32) * D_MODEL ** -0.5,
        'b_gate': 0.02 * nrm(ks[23], (DEPTH, N_EXPERTS, D_FF), F32),
        'w_up': nrm(ks[24], (DEPTH, N_EXPERTS, D_MODEL, D_FF), F32) * D_MODEL ** -0.5,
        'b_up': 0.02 * nrm(ks[25], (DEPTH, N_EXPERTS, D_FF), F32),
        'w_down': nrm(ks[26], (DEPTH, N_EXPERTS, D_FF, D_MODEL), F32) * D_FF ** -0.5,
        'b_down': 0.02 * nrm(ks[27], (DEPTH, N_EXPERTS, D_MODEL), F32),
        'g_final': 1.0 + 0.02 * nrm(ks[28], (D_MODEL,), F32),
    }


def reference(x_prompt, x_sample, state_ssm, state_ssd_conv, state_conf_conv, meta_tokens, g_mix,
              w_in, conv_ssd_w, conv_ssd_b, dt_bias, a_log, d_skip, g_ssd_norm, conv_conf_w,
              conv_conf_b, ln_conf_g, ln_conf_b, w_out, g_ffn, w_router, b_router, w_gate, b_gate,
              w_up, b_up, w_down, b_down, g_final):
    dtype = x_prompt.dtype
    bp = x_prompt.shape[0]
    meta = jnp.broadcast_to(meta_tokens.astype(dtype)[None], (bp, N_META, D_MODEL))
    h_p = jnp.concatenate([meta, x_prompt], axis=1)
    h_s = x_sample
    front_pad = (-N_META) % CHUNK
    ssm_p, ssdc_p, confc_p, ssm_s, ssdc_s, confc_s = [], [], [], [], [], []
    for l in range(DEPTH):
        mix_w = (w_in[l], conv_ssd_w[l], conv_ssd_b[l], dt_bias[l], a_log[l], d_skip[l], g_ssd_norm[l],
                 conv_conf_w[l], conv_conf_b[l], ln_conf_g[l], ln_conf_b[l], w_out[l])
        moe_w = (w_router[l], b_router[l], w_gate[l], b_gate[l], w_up[l], b_up[l], w_down[l], b_down[l])
        ssm0 = jnp.zeros((bp, SSD_HEADS, SSD_HEAD_DIM, SSD_STATE), state_ssm.dtype)
        ssd0 = jnp.zeros((bp, SSD_CONV - 1, SSD_CONV_DIM), dtype)
        conf0 = jnp.zeros((bp, CONF_WIDTH - 1, D_CONF), dtype)
        h_p, s1, s2, s3 = decoder_layer(h_p, ssm0, ssd0, conf0, front_pad, g_mix[l], g_ffn[l], mix_w, moe_w)
        ssm_p.append(s1)
        ssdc_p.append(s2)
        confc_p.append(s3)
        h_s, s1, s2, s3 = decoder_layer(h_s, state_ssm[l], state_ssd_conv[l], state_conf_conv[l], 0,
                                        g_mix[l], g_ffn[l], mix_w, moe_w)
        ssm_s.append(s1)
        ssdc_s.append(s2)
        confc_s.append(s3)
    y_prompt = rmsnorm(h_p, g_final)[:, N_META:]
    y_sample = rmsnorm(h_s, g_final)
    new_ssm_p = jnp.stack(ssm_p, axis=0)
    new_ssdc_p = jnp.stack(ssdc_p, axis=0)
    new_confc_p = jnp.stack(confc_p, axis=0)
    new_ssm_s = jnp.stack(ssm_s, axis=0)
    new_ssdc_s = jnp.stack(ssdc_s, axis=0)
    new_confc_s = jnp.stack(confc_s, axis=0)
    return (y_prompt, y_sample, new_ssm_p, new_ssdc_p, new_confc_p, new_ssm_s, new_ssdc_s, new_confc_s)
```

```python
import functools

import jax
import jax.numpy as jnp
from jax import lax
from jax.experimental import pallas as pl
from jax.experimental.pallas import tpu as pltpu

F32 = jnp.float32
BF16 = jnp.bfloat16
HIGHEST = lax.Precision.HIGHEST

N_META = 16
SSD_HEAD_DIM = 64
SSD_GROUPS = 2
SSD_STATE = 128
SSD_CONV = 4
CHUNK = 128
CONF_WIDTH = 31
N_EXPERTS = 32
TOP_K = 4
SWIGLU_LIMIT = 7.0
SWIGLU_ALPHA = 1.702
EPS = 1e-5

LANES = 128
SUBLANES = 8
MOE_BLOCK = 256
VMEM_LIMIT = 56 * 1024 * 1024


def _cparams(sem):
    return pltpu.CompilerParams(dimension_semantics=sem, vmem_limit_bytes=VMEM_LIMIT)


def _sigmoid(x):
    return 1.0 / (1.0 + jnp.exp(-x))


def _silu(x):
    return x * _sigmoid(x)


def _softplus(x):
    return jnp.maximum(x, 0.0) + jnp.log1p(jnp.exp(-jnp.abs(x)))


def _rms(x, g):
    return x * lax.rsqrt(jnp.mean(x * x, axis=-1, keepdims=True) + EPS) * g


def _inproj_kernel(x_ref, g_ref, wz_ref, wx_ref, wg_ref, wd_ref, z_ref, xbc_ref, glu_ref, dt_ref):
    u = _rms(x_ref[...], g_ref[...]).astype(BF16)
    z_ref[...] = jnp.dot(u, wz_ref[...], preferred_element_type=F32).astype(BF16)
    xbc_ref[...] = jnp.dot(u, wx_ref[...], preferred_element_type=F32).astype(BF16)
    glu_ref[...] = jnp.dot(u, wg_ref[...], preferred_element_type=F32).astype(BF16)
    dt_ref[...] = jnp.dot(u, wd_ref[...], preferred_element_type=F32)


def _inproj(x, g, wz, wx, wg, wd):
    t, d = x.shape
    tm = min(512, t)
    assert t % tm == 0
    row = lambda n: pl.BlockSpec((tm, n), lambda i: (i, 0))
    full = lambda a: pl.BlockSpec(a.shape, lambda i: (0, 0))
    return pl.pallas_call(
        _inproj_kernel,
        grid=(t // tm,),
        in_specs=[row(d), full(g), full(wz), full(wx), full(wg), full(wd)],
        out_specs=[row(wz.shape[1]), row(wx.shape[1]), row(wg.shape[1]), row(wd.shape[1])],
        out_shape=[jax.ShapeDtypeStruct((t, wz.shape[1]), BF16),
                   jax.ShapeDtypeStruct((t, wx.shape[1]), BF16),
                   jax.ShapeDtypeStruct((t, wg.shape[1]), BF16),
                   jax.ShapeDtypeStruct((t, wd.shape[1]), F32)],
        compiler_params=_cparams(("parallel",)),
        name="inproj",
    )(x, g, wz, wx, wg, wd)


def _seq_kernel(xbc_ref, z_ref, glu_ref, dt_ref, xbcm_ref, zm_ref, glum_ref, dtm_ref,
                cw_ref, cb_ref, dtb_ref, alog_ref, dskip_ref, gn_ref,
                ccw_ref, ccb_ref, lng_ref, lnb_ref,
                y_ref, c_ref, ssm_ref, shist_ref, chist_ref,
                st_ref, xbuf_ref, cbuf_ref, *, d_ssm, n_heads, front_pad):
    c = pl.program_id(1)
    last = pl.num_programs(1) - 1
    is_meta = c == 0
    q = CHUNK
    n = SSD_STATE
    gw = SSD_GROUPS * n

    @pl.when(is_meta)
    def _():
        st_ref[...] = jnp.zeros_like(st_ref)
        xbuf_ref[...] = jnp.zeros_like(xbuf_ref)
        cbuf_ref[...] = jnp.zeros_like(cbuf_ref)

    xbc_raw = jnp.where(is_meta, xbcm_ref[...], xbc_ref[0]).astype(F32)
    xbuf_ref[SUBLANES:SUBLANES + q, :] = xbc_raw
    acc = cb_ref[...] + cw_ref[SSD_CONV - 1:SSD_CONV, :] * xbc_raw
    for j in range(1, SSD_CONV):
        acc = acc + cw_ref[SSD_CONV - 1 - j:SSD_CONV - j, :] * xbuf_ref[SUBLANES - j:SUBLANES - j + q, :]
    xbuf_ref[0:SUBLANES, :] = xbuf_ref[q:q + SUBLANES, :]
    xact = _silu(acc)
    xs = xact[:, :d_ssm]
    bm = [xact[:, d_ssm + g * n:d_ssm + (g + 1) * n] for g in range(SSD_GROUPS)]
    cm = [xact[:, d_ssm + gw + g * n:d_ssm + gw + (g + 1) * n] for g in range(SSD_GROUPS)]

    dt_raw = jnp.where(is_meta, dtm_ref[...], dt_ref[0])
    rows = lax.broadcasted_iota(jnp.int32, (q, LANES), 0)
    cols = lax.broadcasted_iota(jnp.int32, (q, LANES), 1)
    valid = jnp.logical_or(jnp.logical_not(is_meta), rows >= front_pad)
    dtv = jnp.where(valid, _softplus(dt_raw + dtb_ref[...]), 0.0)
    da = dtv * (-jnp.exp(alog_ref[...]))
    causal = rows >= cols
    tri = causal.astype(F32)
    a_cum = jnp.dot(tri, da, preferred_element_type=F32, precision=HIGHEST)
    a_cum_t = a_cum.T
    dt_t = dtv.T
    w_t = jnp.exp(a_cum_t[:, q - 1:q] - a_cum_t) * dt_t
    a_last = a_cum[q - 1:q, :]

    bm_t = [b.T for b in bm]
    cb = [lax.dot_general(cm[g].astype(BF16), bm[g].astype(BF16), (((1,), (1,)), ((), ())),
                          preferred_element_type=F32) for g in range(SSD_GROUPS)]
    lane = lax.broadcasted_iota(jnp.int32, (q, LANES), 1)
    lo = lane < SSD_HEAD_DIM
    heads_per_group = n_heads // SSD_GROUPS

    y_parts = []
    for j in range(n_heads // 2):
        g = (2 * j) // heads_per_group
        m_l, ce_l, bw_l = [], [], []
        for h in (2 * j, 2 * j + 1):
            col = a_cum[:, h:h + 1]
            seg = col - a_cum_t[h:h + 1, :]
            dec = jnp.exp(jnp.where(causal, seg, -jnp.inf))
            m_l.append((cb[g] * dec * dt_t[h:h + 1, :]).astype(BF16))
            ce_l.append((cm[g] * jnp.exp(col)).astype(BF16))
            bw_l.append((bm_t[g] * w_t[h:h + 1, :]).astype(BF16))
        sl = slice(j * LANES, (j + 1) * LANES)
        xs_pair = xs[:, sl]
        rhs_x = jnp.concatenate([jnp.where(lo, xs_pair, 0.0), jnp.where(lo, 0.0, xs_pair)],
                                axis=0).astype(BF16)
        st_pair = st_ref[:, sl]
        rhs_s = jnp.concatenate([jnp.where(lo, st_pair, 0.0), jnp.where(lo, 0.0, st_pair)],
                                axis=0).astype(BF16)
        lhs_y = jnp.concatenate(m_l + ce_l, axis=1)
        y_pair = jnp.dot(lhs_y, jnp.concatenate([rhs_x, rhs_s], axis=0), preferred_element_type=F32)
        st_new = jnp.dot(jnp.concatenate(bw_l, axis=1), rhs_x, preferred_element_type=F32)
        cd = jnp.exp(jnp.where(lo[0:1, :], a_last[:, 2 * j:2 * j + 1], a_last[:, 2 * j + 1:2 * j + 2]))
        st_ref[:, sl] = st_pair * cd + st_new
        y_parts.append(y_pair + dskip_ref[:, sl] * xs_pair)

    y = jnp.concatenate(y_parts, axis=1)
    zf = jnp.where(is_meta, zm_ref[...], z_ref[0]).astype(F32)
    y = y * _silu(zf)
    gsz = d_ssm // SSD_GROUPS
    y = jnp.concatenate(
        [_rms(y[:, g * gsz:(g + 1) * gsz], gn_ref[:, g * gsz:(g + 1) * gsz]) for g in range(SSD_GROUPS)],
        axis=1)
    y_ref[0] = y.astype(BF16)

    glu = jnp.where(is_meta, glum_ref[...], glu_ref[0]).astype(F32)
    dc = glu.shape[1] // 2
    cval = glu[:, :dc] * _sigmoid(glu[:, dc:])
    hist = CONF_WIDTH - 1
    base = 32
    cbuf_ref[base:base + q, :] = cval
    cacc = ccb_ref[...] + ccw_ref[hist:hist + 1, :] * cval
    for k in range(hist):
        off = base - hist + k
        cacc = cacc + ccw_ref[k:k + 1, :] * cbuf_ref[off:off + q, :]
    cbuf_ref[0:base, :] = cbuf_ref[q:q + base, :]
    mu = jnp.mean(cacc, axis=-1, keepdims=True)
    var = jnp.mean(jnp.square(cacc - mu), axis=-1, keepdims=True)
    cn = (cacc - mu) * lax.rsqrt(var + EPS) * lng_ref[...] + lnb_ref[...]
    c_ref[0] = _silu(cn).astype(BF16)

    @pl.when(c == last)
    def _():
        ssm_ref[0] = st_ref[...].T.reshape(n_heads, SSD_HEAD_DIM, n)
        shist_ref[0] = xbuf_ref[SUBLANES - (SSD_CONV - 1):SUBLANES, :]
        chist_ref[0] = cbuf_ref[base - hist:base, :]


def _seq(xbc, z, glu, dt, xbcm, zm, glum, dtm, params, *, n_heads):
    b, seq, conv_dim = xbc.shape
    d_ssm = z.shape[2]
    dc = glu.shape[2] // 2
    nc = seq // CHUNK + 1
    front_pad = (-N_META) % CHUNK
    blk = lambda n: pl.BlockSpec((1, CHUNK, n), lambda i, c: (i, jnp.maximum(c - 1, 0), 0))
    full = lambda a: pl.BlockSpec(a.shape, lambda i, c: (0,) * a.ndim)
    kern = functools.partial(_seq_kernel, d_ssm=d_ssm, n_heads=n_heads, front_pad=front_pad)
    return pl.pallas_call(
        kern,
        grid=(b, nc),
        in_specs=[blk(conv_dim), blk(d_ssm), blk(2 * dc), blk(LANES),
                  full(xbcm), full(zm), full(glum), full(dtm)] + [full(p) for p in params],
        out_specs=[blk(d_ssm), blk(dc),
                   pl.BlockSpec((1, n_heads, SSD_HEAD_DIM, SSD_STATE), lambda i, c: (i, 0, 0, 0)),
                   pl.BlockSpec((1, SSD_CONV - 1, conv_dim), lambda i, c: (i, 0, 0)),
                   pl.BlockSpec((1, CONF_WIDTH - 1, dc), lambda i, c: (i, 0, 0))],
        out_shape=[jax.ShapeDtypeStruct((b, seq, d_ssm), BF16),
                   jax.ShapeDtypeStruct((b, seq, dc), BF16),
                   jax.ShapeDtypeStruct((b, n_heads, SSD_HEAD_DIM, SSD_STATE), F32),
                   jax.ShapeDtypeStruct((b, SSD_CONV - 1, conv_dim), F32),
                   jax.ShapeDtypeStruct((b, CONF_WIDTH - 1, dc), F32)],
        scratch_shapes=[pltpu.VMEM((SSD_STATE, d_ssm), F32),
                        pltpu.VMEM((SUBLANES + CHUNK, conv_dim), F32),
                        pltpu.VMEM((32 + CHUNK, dc), F32)],
        compiler_params=_cparams(("parallel", "arbitrary")),
        name="seq",
    )(xbc, z, glu, dt, xbcm, zm, glum, dtm, *params)


def _sample_kernel(xbc_ref, z_ref, glu_ref, dt_ref, ssm_ref, shist_ref, chist_ref,
                   cw_ref, cb_ref, dtb_ref, alog_ref, dskip_ref, gn_ref,
                   ccw_ref, ccb_ref, lng_ref, lnb_ref,
                   y_ref, c_ref, ssm_o, shist_o, chist_o, *, d_ssm, n_heads):
    sb = xbc_ref.shape[0]
    n = SSD_STATE
    gw = SSD_GROUPS * n
    hp = d_ssm // SSD_GROUPS
    heads_per_group = n_heads // SSD_GROUPS

    x_new = xbc_ref[...].astype(F32)
    acc = cb_ref[...] + cw_ref[SSD_CONV - 1:SSD_CONV, :] * x_new
    for k in range(SSD_CONV - 1):
        acc = acc + cw_ref[k:k + 1, :] * shist_ref[:, k, :]
    for k in range(SSD_CONV - 2):
        shist_o[:, k, :] = shist_ref[:, k + 1, :]
    shist_o[:, SSD_CONV - 2, :] = x_new
    xact = _silu(acc)
    xs = xact[:, :d_ssm]
    dtv = _softplus(dt_ref[...] + dtb_ref[...])
    a = -jnp.exp(alog_ref[...])
    decay = jnp.exp(dtv * a)

    hsel = (lax.broadcasted_iota(jnp.int32, (LANES, d_ssm), 1) // SSD_HEAD_DIM
            == lax.broadcasted_iota(jnp.int32, (LANES, d_ssm), 0)).astype(F32)
    dt_x = jnp.dot(dtv, hsel, preferred_element_type=F32, precision=HIGHEST)
    xdt = (xs * dt_x).astype(BF16)
    rowid = lax.broadcasted_iota(jnp.int32, (sb, 1), 0)

    y_rows = []
    for i in range(sb):
        sel = rowid == i
        xi = jnp.where(sel, xdt, jnp.zeros_like(xdt))
        parts = []
        for g in range(SSD_GROUPS):
            bm = xact[:, d_ssm + g * n:d_ssm + (g + 1) * n].astype(BF16)
            cmat = xact[:, d_ssm + gw + g * n:d_ssm + gw + (g + 1) * n].astype(BF16)
            outer = lax.dot_general(xi[:, g * hp:(g + 1) * hp], bm, (((0,), (0,)), ((), ())),
                                    preferred_element_type=F32)
            outer = outer.reshape(heads_per_group, SSD_HEAD_DIM, n)
            hs = slice(g * heads_per_group, (g + 1) * heads_per_group)
            s_old = ssm_ref[i, hs]
            dec_i = jnp.stack([jnp.broadcast_to(decay[i:i + 1, h:h + 1], (SSD_HEAD_DIM, n))
                               for h in range(g * heads_per_group, (g + 1) * heads_per_group)])
            s_new = s_old * dec_i + outer
            ssm_o[i, hs] = s_new
            yv = lax.dot_general(cmat, s_new.reshape(hp, n).astype(BF16), (((1,), (1,)), ((), ())),
                                 preferred_element_type=F32)
            parts.append(yv)
        yfull = jnp.concatenate(parts, axis=1)
        y_rows.append(jnp.where(sel, yfull, 0.0))
    y = y_rows[0]
    for r in y_rows[1:]:
        y = y + r
    y = y + dskip_ref[...] * xs
    y = y * _silu(z_ref[...].astype(F32))
    gsz = d_ssm // SSD_GROUPS
    y = jnp.concatenate(
        [_rms(y[:, g * gsz:(g + 1) * gsz], gn_ref[:, g * gsz:(g + 1) * gsz]) for g in range(SSD_GROUPS)],
        axis=1)
    y_ref[...] = y.astype(BF16)

    glu = glu_ref[...].astype(F32)
    dc = glu.shape[1] // 2
    cval = glu[:, :dc] * _sigmoid(glu[:, dc:])
    hist = CONF_WIDTH - 1
    cacc = ccb_ref[...] + ccw_ref[hist:hist + 1, :] * cval
    cacc = cacc + jnp.sum(chist_ref[...] * ccw_ref[0:hist, :][None], axis=1)
    chist_o[:, 0:hist - 1, :] = chist_ref[:, 1:hist, :]
    chist_o[:, hist - 1, :] = cval
    mu = jnp.mean(cacc, axis=-1, keepdims=True)
    var = jnp.mean(jnp.square(cacc - mu), axis=-1, keepdims=True)
    cn = (cacc - mu) * lax.rsqrt(var + EPS) * lng_ref[...] + lnb_ref[...]
    c_ref[...] = _silu(cn).astype(BF16)


def _sample(xbc, z, glu, dt, ssm, shist, chist, params, *, n_heads):
    bs, conv_dim = xbc.shape
    d_ssm = z.shape[1]
    dc = glu.shape[1] // 2
    sb = SUBLANES
    assert bs % sb == 0
    row = lambda n: pl.BlockSpec((sb, n), lambda i: (i, 0))
    full = lambda a: pl.BlockSpec(a.shape, lambda i: (0,) * a.ndim)
    st_spec = pl.BlockSpec((sb, n_heads, SSD_HEAD_DIM, SSD_STATE), lambda i: (i, 0, 0, 0))
    sh_spec = pl.BlockSpec((sb, SSD_CONV - 1, conv_dim), lambda i: (i, 0, 0))
    ch_spec = pl.BlockSpec((sb, CONF_WIDTH - 1, dc), lambda i: (i, 0, 0))
    kern = functools.partial(_sample_kernel, d_ssm=d_ssm, n_heads=n_heads)
    return pl.pallas_call(
        kern,
        grid=(bs // sb,),
        in_specs=[row(conv_dim), row(d_ssm), row(2 * dc), row(LANES), st_spec, sh_spec, ch_spec]
        + [full(p) for p in params],
        out_specs=[row(d_ssm), row(dc), st_spec, sh_spec, ch_spec],
        out_shape=[jax.ShapeDtypeStruct((bs, d_ssm), BF16),
                   jax.ShapeDtypeStruct((bs, dc), BF16),
                   jax.ShapeDtypeStruct(ssm.shape, F32),
                   jax.ShapeDtypeStruct(shist.shape, F32),
                   jax.ShapeDtypeStruct(chist.shape, F32)],
        compiler_params=_cparams(("parallel",)),
        name="sample_step",
    )(xbc, z, glu, dt, ssm, shist, chist, *params)


def _outproj_kernel(y_ref, c_ref, res_ref, wy_ref, wc_ref, g_ref, wr_ref, br_ref, cnt_in_ref,
                    h_ref, xn_ref, idx_ref, gate_ref, rank_ref, cnt_ref, carry_ref):
    i = pl.program_id(0)

    @pl.when(i == 0)
    def _():
        carry_ref[...] = cnt_in_ref[...]

    h = (res_ref[...]
         + jnp.dot(y_ref[...], wy_ref[...], preferred_element_type=F32)
         + jnp.dot(c_ref[...], wc_ref[...], preferred_element_type=F32))
    h_ref[...] = h
    xn = _rms(h, g_ref[...])
    xn_ref[...] = xn
    tm = h.shape[0]
    logits = lax.dot_general(wr_ref[...], xn, (((1,), (1,)), ((), ())),
                             preferred_element_type=F32, precision=HIGHEST) + br_ref[...]
    eid = lax.broadcasted_iota(jnp.int32, logits.shape, 0)
    vals, idxs, hots = [], [], []
    work = logits
    for _ in range(TOP_K):
        m = jnp.max(work, axis=0, keepdims=True)
        sel = jnp.min(jnp.where(work == m, eid, N_EXPERTS), axis=0, keepdims=True)
        hot = eid == sel
        work = jnp.where(hot, -jnp.inf, work)
        vals.append(m)
        idxs.append(sel)
        hots.append(hot)
    exps = [jnp.exp(v - vals[0]) for v in vals]
    den = exps[0]
    for e in exps[1:]:
        den = den + e
    chosen = hots[0]
    for hot in hots[1:]:
        chosen = jnp.logical_or(chosen, hot)
    chosen_f = chosen.astype(F32)
    r = lax.broadcasted_iota(jnp.int32, (tm, tm), 0)
    cidx = lax.broadcasted_iota(jnp.int32, (tm, tm), 1)
    upper = (r < cidx).astype(BF16)
    cum = jnp.dot(chosen_f.astype(BF16), upper, preferred_element_type=F32) + carry_ref[:, 0:1]
    for k in range(TOP_K):
        idx_ref[k:k + 1, :] = idxs[k]
        gate_ref[k:k + 1, :] = exps[k] / den
        rank_ref[k:k + 1, :] = jnp.sum(jnp.where(hots[k], cum, 0.0), axis=0, keepdims=True).astype(jnp.int32)
    carry_ref[...] = carry_ref[...] + jnp.sum(chosen_f, axis=1, keepdims=True)
    cnt_ref[...] = carry_ref[...]


def _outproj(y, c, res, wy, wc, g, wr_t, br, cnt_in):
    t, d = res.shape
    tm = min(512, t)
    assert t % tm == 0
    row = lambda n: pl.BlockSpec((tm, n), lambda i: (i, 0))
    col = pl.BlockSpec((TOP_K, tm), lambda i: (0, i))
    full = lambda a: pl.BlockSpec(a.shape, lambda i: (0,) * a.ndim)
    return pl.pallas_call(
        _outproj_kernel,
        grid=(t // tm,),
        in_specs=[row(y.shape[1]), row(c.shape[1]), row(d), full(wy), full(wc), full(g), full(wr_t),
                  full(br), full(cnt_in)],
        out_specs=[row(d), row(d), col, col, col, full(cnt_in)],
        out_shape=[jax.ShapeDtypeStruct((t, d), F32),
                   jax.ShapeDtypeStruct((t, d), F32),
                   jax.ShapeDtypeStruct((TOP_K, t), jnp.int32),
                   jax.ShapeDtypeStruct((TOP_K, t), F32),
                   jax.ShapeDtypeStruct((TOP_K, t), jnp.int32),
                   jax.ShapeDtypeStruct(cnt_in.shape, F32)],
        scratch_shapes=[pltpu.VMEM(cnt_in.shape, F32)],
        compiler_params=_cparams(("arbitrary",)),
        name="outproj_router",
    )(y, c, res, wy, wc, g, wr_t, br, cnt_in)


def _dispatch_kernel(dest_ref, xn_ref, xs_in_ref, xs_ref, sem):
    del xs_in_ref
    i = pl.program_id(0)
    tm = xn_ref.shape[0]

    def row_copy(t, k):
        d = dest_ref[(i * tm + t) * TOP_K + k]
        return pltpu.make_async_copy(xn_ref.at[pl.ds(t, 1), :], xs_ref.at[pl.ds(d, 1), :], sem)

    def issue(t, carry):
        for k in range(TOP_K):
            row_copy(t, k).start()
        return carry

    lax.fori_loop(0, tm, issue, 0)

    def drain(t, carry):
        for k in range(TOP_K):
            row_copy(t, k).wait()
        return carry

    lax.fori_loop(0, tm, drain, 0)


def _dispatch(dest_flat, xn, xs):
    t, d = xn.shape
    tm = min(512, t)
    assert t % tm == 0
    return pl.pallas_call(
        _dispatch_kernel,
        grid_spec=pltpu.PrefetchScalarGridSpec(
            num_scalar_prefetch=1,
            grid=(t // tm,),
            in_specs=[pl.BlockSpec((tm, d), lambda i, dest: (i, 0)),
                      pl.BlockSpec(memory_space=pl.ANY)],
            out_specs=pl.BlockSpec(memory_space=pl.ANY),
            scratch_shapes=[pltpu.SemaphoreType.DMA(())],
        ),
        out_shape=jax.ShapeDtypeStruct(xs.shape, xs.dtype),
        input_output_aliases={2: 0},
        compiler_params=_cparams(("arbitrary",)),
        name="dispatch",
    )(dest_flat, xn, xs)


def _moe_kernel(be_ref, nb_ref, x_ref, wg_ref, bg_ref, wu_ref, bu_ref, wd_ref, bd_ref, o_ref,
                wg_s, wu_s, wd_s):
    i = pl.program_id(0)
    prev = be_ref[jnp.maximum(i - 1, 0)]
    new_expert = jnp.logical_or(i == 0, be_ref[i] != prev)
    active = i < nb_ref[0]

    @pl.when(jnp.logical_and(active, new_expert))
    def _():
        wg_s[...] = wg_ref[0].astype(BF16)
        wu_s[...] = wu_ref[0].astype(BF16)
        wd_s[...] = wd_ref[0].astype(BF16)

    @pl.when(active)
    def _():
        x = x_ref[...].astype(BF16)
        gt = jnp.minimum(jnp.dot(x, wg_s[...], preferred_element_type=F32) + bg_ref[0], SWIGLU_LIMIT)
        up = jnp.clip(jnp.dot(x, wu_s[...], preferred_element_type=F32) + bu_ref[0],
                      -SWIGLU_LIMIT, SWIGLU_LIMIT)
        hdn = (up + 1.0) * gt * _sigmoid(gt * SWIGLU_ALPHA)
        o_ref[...] = jnp.dot(hdn.astype(BF16), wd_s[...], preferred_element_type=F32) + bd_ref[0]

    @pl.when(jnp.logical_not(active))
    def _():
        o_ref[...] = jnp.zeros_like(o_ref)


def _moe(block_expert, n_active, xs, w_gate, b_gate, w_up, b_up, w_down, b_down):
    rows, d = xs.shape
    nb = rows // MOE_BLOCK
    dff = w_gate.shape[2]
    wspec = lambda a: pl.BlockSpec((1,) + a.shape[1:], lambda i, be, na: (be[i], 0, 0))
    xspec = pl.BlockSpec((MOE_BLOCK, d), lambda i, be, na: (jnp.minimum(i, na[0] - 1), 0))
    return pl.pallas_call(
        _moe_kernel,
        grid_spec=pltpu.PrefetchScalarGridSpec(
            num_scalar_prefetch=2,
            grid=(nb,),
            in_specs=[xspec, wspec(w_gate), wspec(b_gate), wspec(w_up), wspec(b_up),
                      wspec(w_down), wspec(b_down)],
            out_specs=pl.BlockSpec((MOE_BLOCK, d), lambda i, be, na: (i, 0)),
            scratch_shapes=[pltpu.VMEM((d, dff), BF16), pltpu.VMEM((d, dff), BF16),
                            pltpu.VMEM((dff, d), BF16)],
        ),
        out_shape=jax.ShapeDtypeStruct((rows, d), F32),
        compiler_params=_cparams(("arbitrary",)),
        name="moe_experts",
    )(block_expert, n_active, xs, w_gate, b_gate, w_up, b_up, w_down, b_down)


def _combine_kernel(dest_ref, h_ref, gate_ref, gf_ref, yb_ref, o_ref, buf, sem):
    i = pl.program_id(0)
    tm = h_ref.shape[0]

    def row_copy(t, k):
        d = dest_ref[(i * tm + t) * TOP_K + k]
        return pltpu.make_async_copy(yb_ref.at[pl.ds(d, 1), :], buf.at[k, pl.ds(t, 1), :], sem)

    def issue(t, carry):
        for k in range(TOP_K):
            row_copy(t, k).start()
        return carry

    lax.fori_loop(0, tm, issue, 0)

    def drain(t, carry):
        for k in range(TOP_K):
            row_copy(t, k).wait()
        return carry

    lax.fori_loop(0, tm, drain, 0)

    acc = h_ref[...]
    for k in range(TOP_K):
        acc = acc + gate_ref[:, k:k + 1] * buf[k]
    o_ref[...] = _rms(acc, gf_ref[...])


def _combine(dest_flat, h, gates, g_final, yb):
    t, d = h.shape
    tm = min(256, t)
    assert t % tm == 0
    return pl.pallas_call(
        _combine_kernel,
        grid_spec=pltpu.PrefetchScalarGridSpec(
            num_scalar_prefetch=1,
            grid=(t // tm,),
            in_specs=[pl.BlockSpec((tm, d), lambda i, dest: (i, 0)),
                      pl.BlockSpec((tm, TOP_K), lambda i, dest: (i, 0)),
                      pl.BlockSpec(g_final.shape, lambda i, dest: (0, 0)),
                      pl.BlockSpec(memory_space=pl.ANY)],
            out_specs=pl.BlockSpec((tm, d), lambda i, dest: (i, 0)),
            scratch_shapes=[pltpu.VMEM((TOP_K, tm, d), F32), pltpu.SemaphoreType.DMA(())],
        ),
        out_shape=jax.ShapeDtypeStruct((t, d), F32),
        compiler_params=_cparams(("arbitrary",)),
        name="combine",
    )(dest_flat, h, gates, g_final, yb)


def _pad_lanes(v, fill=0.0):
    v = v.reshape(1, -1).astype(F32)
    return jnp.pad(v, ((0, 0), (0, LANES - v.shape[1])), constant_values=fill)


def kernel(x_prompt, x_sample, state_ssm, state_ssd_conv, state_conf_conv, meta_tokens, g_mix, w_in,
           conv_ssd_w, conv_ssd_b, dt_bias, a_log, d_skip, g_ssd_norm, conv_conf_w, conv_conf_b,
           ln_conf_g, ln_conf_b, w_out, g_ffn, w_router, b_router, w_gate, b_gate, w_up, b_up,
           w_down, b_down, g_final):
    bp, seq, d = x_prompt.shape
    bs = x_sample.shape[0]
    depth = w_in.shape[0]
    assert depth == 1 and x_sample.shape[1] == 1 and seq % CHUNK == 0
    n_heads = a_log.shape[1]
    d_ssm = n_heads * SSD_HEAD_DIM
    conv_dim = conv_ssd_w.shape[2]
    dc = conv_conf_w.shape[2]
    l = 0

    o1, o2, o3 = d_ssm, d_ssm + conv_dim, d_ssm + conv_dim + n_heads
    w_in_l = w_in[l]
    wz = w_in_l[:, :o1].astype(BF16)
    wx = w_in_l[:, o1:o2].astype(BF16)
    wdt = jnp.pad(w_in_l[:, o2:o3], ((0, 0), (0, LANES - n_heads))).astype(BF16)
    wglu = w_in_l[:, o3:].astype(BF16)
    row2 = lambda v: v.reshape(1, -1).astype(F32)
    seq_params = (
        conv_ssd_w[l], row2(conv_ssd_b[l]), _pad_lanes(dt_bias[l]), _pad_lanes(a_log[l]),
        row2(jnp.repeat(d_skip[l], SSD_HEAD_DIM)), row2(g_ssd_norm[l]),
        jnp.pad(conv_conf_w[l], ((0, 32 - CONF_WIDTH), (0, 0))), row2(conv_conf_b[l]),
        row2(ln_conf_g[l]), row2(ln_conf_b[l]))
    g_mix_r = row2(g_mix[l])

    xp2 = x_prompt.reshape(bp * seq, d)
    front_pad = (-N_META) % CHUNK
    small = jnp.concatenate([jnp.zeros((front_pad, d), F32), meta_tokens.astype(F32),
                             x_sample.reshape(bs, d)], axis=0)
    z_p, xbc_p, glu_p, dt_p = _inproj(xp2, g_mix_r, wz, wx, wglu, wdt)
    z_m, xbc_m, glu_m, dt_m = _inproj(small, g_mix_r, wz, wx, wglu, wdt)

    r3 = lambda a: a.reshape(bp, seq, a.shape[-1])
    y_p, c_p, ssm_p, shist_p, chist_p = _seq(
        r3(xbc_p), r3(z_p), r3(glu_p), r3(dt_p),
        xbc_m[:CHUNK], z_m[:CHUNK], glu_m[:CHUNK], dt_m[:CHUNK], seq_params, n_heads=n_heads)

    y_s, c_s, ssm_s, shist_s, chist_s = _sample(
        xbc_m[CHUNK:], z_m[CHUNK:], glu_m[CHUNK:], dt_m[CHUNK:],
        state_ssm[l], state_ssd_conv[l], state_conf_conv[l], seq_params, n_heads=n_heads)

    w_out_l = w_out[l]
    wy = w_out_l[:d_ssm].astype(BF16)
    wc = w_out_l[d_ssm:].astype(BF16)
    g_ffn_r = row2(g_ffn[l])
    wr_t = w_router[l].T.astype(F32)
    br = b_router[l].reshape(N_EXPERTS, 1).astype(F32)
    cnt0 = jnp.zeros((N_EXPERTS, LANES), F32)
    h_p, xn_p, idx_p, gate_p, rank_p, cnt1 = _outproj(
        y_p.reshape(bp * seq, d_ssm), c_p.reshape(bp * seq, dc), xp2, wy, wc, g_ffn_r, wr_t, br, cnt0)
    h_s, xn_s, idx_s, gate_s, rank_s, cnt2 = _outproj(
        y_s, c_s, x_sample.reshape(bs, d), wy, wc, g_ffn_r, wr_t, br, cnt1)

    counts = cnt2[:, 0].astype(jnp.int32)
    padded = (counts + MOE_BLOCK - 1) // MOE_BLOCK * MOE_BLOCK
    pad_end = jnp.cumsum(padded)
    pad_start = pad_end - padded
    t_all = bp * seq + bs
    n_blocks = -(-(t_all * TOP_K) // MOE_BLOCK) + N_EXPERTS
    rows = n_blocks * MOE_BLOCK
    block_expert = jnp.minimum(
        jnp.searchsorted(pad_end, jnp.arange(n_blocks, dtype=jnp.int32) * MOE_BLOCK, side='right'),
        N_EXPERTS - 1).astype(jnp.int32)
    n_active = (pad_end[-1] // MOE_BLOCK).astype(jnp.int32).reshape(1)
    dest_p = (pad_start[idx_p] + rank_p).T.reshape(-1).astype(jnp.int32)
    dest_s = (pad_start[idx_s] + rank_s).T.reshape(-1).astype(jnp.int32)

    xs = jnp.zeros((rows, d), F32)
    xs = _dispatch(dest_p, xn_p, xs)
    xs = _dispatch(dest_s, xn_s, xs)
    yb = _moe(block_expert, n_active, xs, w_gate[l], b_gate[l].reshape(N_EXPERTS, 1, -1),
              w_up[l], b_up[l].reshape(N_EXPERTS, 1, -1), w_down[l], b_down[l].reshape(N_EXPERTS, 1, -1))
    g_fin = row2(g_final)
    y_prompt = _combine(dest_p, h_p, gate_p.T, g_fin, yb).reshape(bp, seq, d)
    y_sample = _combine(dest_s, h_s, gate_s.T, g_fin, yb).reshape(bs, 1, d)

    return (y_prompt, y_sample, ssm_p[None], shist_p[None], chist_p[None],
            ssm_s[None], shist_s[None], chist_s[None])
```

```python
import functools

import jax
import jax.numpy as jnp
from jax import lax
from jax.experimental import pallas as pl
from jax.experimental.pallas import tpu as pltpu

F32 = jnp.float32
BF16 = jnp.bfloat16
HIGHEST = lax.Precision.HIGHEST

N_META = 16
SSD_HEAD_DIM = 64
SSD_GROUPS = 2
SSD_STATE = 128
SSD_CONV = 4
CHUNK = 128
CONF_WIDTH = 31
N_EXPERTS = 32
TOP_K = 4
SWIGLU_LIMIT = 7.0
SWIGLU_ALPHA = 1.702
EPS = 1e-5

LANES = 128
LANE_BITS = LANES.bit_length() - 1
SUBLANES = 8
MOE_BLOCK = 256
VMEM_LIMIT = 56 * 1024 * 1024


def _cparams(sem):
    return pltpu.CompilerParams(dimension_semantics=sem, vmem_limit_bytes=VMEM_LIMIT)


def _sigmoid(x):
    return 1.0 / (1.0 + jnp.exp(-x))


def _tiles_to_rows(ref, n):
    return jnp.concatenate([ref[pl.ds(s, n, stride=SUBLANES), :] for s in range(SUBLANES)], axis=1)


def _rows_to_tiles(ref, val):
    n = val.shape[0]
    for s in range(SUBLANES):
        ref[pl.ds(s, n, stride=SUBLANES), :] = val[:, s * LANES:(s + 1) * LANES]


def _silu(x):
    return x * _sigmoid(x)


def _softplus(x):
    return jnp.maximum(x, 0.0) + jnp.log1p(jnp.exp(-jnp.abs(x)))


def _rms(x, g):
    return x * lax.rsqrt(jnp.mean(x * x, axis=-1, keepdims=True) + EPS) * g


def _inproj_kernel(x_ref, g_ref, wz_ref, wx_ref, wg_ref, wd_ref, z_ref, xbc_ref, glu_ref, dt_ref):
    u = _rms(x_ref[...], g_ref[...]).astype(BF16)
    z_ref[...] = jnp.dot(u, wz_ref[...], preferred_element_type=F32).astype(BF16)
    xbc_ref[...] = jnp.dot(u, wx_ref[...], preferred_element_type=F32).astype(BF16)
    glu_ref[...] = jnp.dot(u, wg_ref[...], preferred_element_type=F32).astype(BF16)
    dt_ref[...] = jnp.dot(u, wd_ref[...], preferred_element_type=F32)


def _inproj(x, g, wz, wx, wg, wd):
    t, d = x.shape
    tm = min(512, t)
    assert t % tm == 0
    row = lambda n: pl.BlockSpec((tm, n), lambda i: (i, 0))
    full = lambda a: pl.BlockSpec(a.shape, lambda i: (0, 0))
    return pl.pallas_call(
        _inproj_kernel,
        grid=(t // tm,),
        in_specs=[row(d), full(g), full(wz), full(wx), full(wg), full(wd)],
        out_specs=[row(wz.shape[1]), row(wx.shape[1]), row(wg.shape[1]), row(wd.shape[1])],
        out_shape=[jax.ShapeDtypeStruct((t, wz.shape[1]), BF16),
                   jax.ShapeDtypeStruct((t, wx.shape[1]), BF16),
                   jax.ShapeDtypeStruct((t, wg.shape[1]), BF16),
                   jax.ShapeDtypeStruct((t, wd.shape[1]), F32)],
        compiler_params=_cparams(("parallel",)),
        name="inproj",
    )(x, g, wz, wx, wg, wd)


def _seq_kernel(xbc_ref, z_ref, glu_ref, dt_ref, xbcm_ref, zm_ref, glum_ref, dtm_ref,
                cw_ref, cb_ref, dtb_ref, alog_ref, dskip_ref, gn_ref,
                ccw_ref, ccb_ref, lng_ref, lnb_ref,
                y_ref, c_ref, ssm_ref, shist_ref, chist_ref,
                st_ref, xbuf_ref, cbuf_ref, *, d_ssm, n_heads, front_pad):
    c = pl.program_id(1)
    last = pl.num_programs(1) - 1
    is_meta = c == 0
    q = CHUNK
    n = SSD_STATE
    gw = SSD_GROUPS * n

    @pl.when(is_meta)
    def _():
        st_ref[...] = jnp.zeros_like(st_ref)
        xbuf_ref[...] = jnp.zeros_like(xbuf_ref)
        cbuf_ref[...] = jnp.zeros_like(cbuf_ref)

    xbc_raw = jnp.where(is_meta, xbcm_ref[...], xbc_ref[0]).astype(F32)
    xbuf_ref[SUBLANES:SUBLANES + q, :] = xbc_raw
    acc = cb_ref[...] + cw_ref[SSD_CONV - 1:SSD_CONV, :] * xbc_raw
    for j in range(1, SSD_CONV):
        acc = acc + cw_ref[SSD_CONV - 1 - j:SSD_CONV - j, :] * xbuf_ref[SUBLANES - j:SUBLANES - j + q, :]
    xbuf_ref[0:SUBLANES, :] = xbuf_ref[q:q + SUBLANES, :]
    xact = _silu(acc)
    xs = xact[:, :d_ssm]
    bm = [xact[:, d_ssm + g * n:d_ssm + (g + 1) * n] for g in range(SSD_GROUPS)]
    cm = [xact[:, d_ssm + gw + g * n:d_ssm + gw + (g + 1) * n] for g in range(SSD_GROUPS)]

    dt_raw = jnp.where(is_meta, dtm_ref[...], dt_ref[0])
    rows = lax.broadcasted_iota(jnp.int32, (q, LANES), 0)
    cols = lax.broadcasted_iota(jnp.int32, (q, LANES), 1)
    valid = jnp.logical_or(jnp.logical_not(is_meta), rows >= front_pad)
    dtv = jnp.where(valid, _softplus(dt_raw + dtb_ref[...]), 0.0)
    da = dtv * (-jnp.exp(alog_ref[...]))
    causal = rows >= cols
    tri = causal.astype(F32)
    a_cum = jnp.dot(tri, da, preferred_element_type=F32, precision=HIGHEST)
    a_cum_t = a_cum.T
    dt_t = dtv.T
    w_t = jnp.exp(a_cum_t[:, q - 1:q] - a_cum_t) * dt_t
    a_last = a_cum[q - 1:q, :]

    bm_t = [b.T for b in bm]
    cb = [lax.dot_general(cm[g].astype(BF16), bm[g].astype(BF16), (((1,), (1,)), ((), ())),
                          preferred_element_type=F32) for g in range(SSD_GROUPS)]
    lane = lax.broadcasted_iota(jnp.int32, (q, LANES), 1)
    lo = lane < SSD_HEAD_DIM
    heads_per_group = n_heads // SSD_GROUPS

    y_parts = []
    for j in range(n_heads // 2):
        g = (2 * j) // heads_per_group
        m_l, ce_l, bw_l = [], [], []
        for h in (2 * j, 2 * j + 1):
            col = a_cum[:, h:h + 1]
            seg = col - a_cum_t[h:h + 1, :]
            dec = jnp.exp(jnp.where(causal, seg, -jnp.inf))
            m_l.append((cb[g] * dec * dt_t[h:h + 1, :]).astype(BF16))
            ce_l.append((cm[g] * jnp.exp(col)).astype(BF16))
            bw_l.append((bm_t[g] * w_t[h:h + 1, :]).astype(BF16))
        sl = slice(j * LANES, (j + 1) * LANES)
        xs_pair = xs[:, sl]
        rhs_x = jnp.concatenate([jnp.where(lo, xs_pair, 0.0), jnp.where(lo, 0.0, xs_pair)],
                                axis=0).astype(BF16)
        st_pair = st_ref[:, sl]
        rhs_s = jnp.concatenate([jnp.where(lo, st_pair, 0.0), jnp.where(lo, 0.0, st_pair)],
                                axis=0).astype(BF16)
        lhs_y = jnp.concatenate(m_l + ce_l, axis=1)
        y_pair = jnp.dot(lhs_y, jnp.concatenate([rhs_x, rhs_s], axis=0), preferred_element_type=F32)
        st_new = jnp.dot(jnp.concatenate(bw_l, axis=1), rhs_x, preferred_element_type=F32)
        cd = jnp.exp(jnp.where(lo[0:1, :], a_last[:, 2 * j:2 * j + 1], a_last[:, 2 * j + 1:2 * j + 2]))
        st_ref[:, sl] = st_pair * cd + st_new
        y_parts.append(y_pair + dskip_ref[:, sl] * xs_pair)

    y = jnp.concatenate(y_parts, axis=1)
    zf = jnp.where(is_meta, zm_ref[...], z_ref[0]).astype(F32)
    y = y * _silu(zf)
    gsz = d_ssm // SSD_GROUPS
    y = jnp.concatenate(
        [_rms(y[:, g * gsz:(g + 1) * gsz], gn_ref[:, g * gsz:(g + 1) * gsz]) for g in range(SSD_GROUPS)],
        axis=1)
    y_ref[0] = y.astype(BF16)

    glu = jnp.where(is_meta, glum_ref[...], glu_ref[0]).astype(F32)
    dc = glu.shape[1] // 2
    cval = glu[:, :dc] * _sigmoid(glu[:, dc:])
    hist = CONF_WIDTH - 1
    base = 32
    cbuf_ref[base:base + q, :] = cval
    cacc = ccb_ref[...] + ccw_ref[hist:hist + 1, :] * cval
    for k in range(hist):
        off = base - hist + k
        cacc = cacc + ccw_ref[k:k + 1, :] * cbuf_ref[off:off + q, :]
    cbuf_ref[0:base, :] = cbuf_ref[q:q + base, :]
    mu = jnp.mean(cacc, axis=-1, keepdims=True)
    var = jnp.mean(jnp.square(cacc - mu), axis=-1, keepdims=True)
    cn = (cacc - mu) * lax.rsqrt(var + EPS) * lng_ref[...] + lnb_ref[...]
    c_ref[0] = _silu(cn).astype(BF16)

    @pl.when(c == last)
    def _():
        ssm_ref[0] = st_ref[...].T.reshape(n_heads, SSD_HEAD_DIM, n)
        shist_ref[0] = xbuf_ref[SUBLANES - (SSD_CONV - 1):SUBLANES, :]
        chist_ref[0] = cbuf_ref[base - hist:base, :]


def _seq(xbc, z, glu, dt, xbcm, zm, glum, dtm, params, *, n_heads):
    b, seq, conv_dim = xbc.shape
    d_ssm = z.shape[2]
    dc = glu.shape[2] // 2
    nc = seq // CHUNK + 1
    front_pad = (-N_META) % CHUNK
    blk = lambda n: pl.BlockSpec((1, CHUNK, n), lambda i, c: (i, jnp.maximum(c - 1, 0), 0))
    full = lambda a: pl.BlockSpec(a.shape, lambda i, c: (0,) * a.ndim)
    kern = functools.partial(_seq_kernel, d_ssm=d_ssm, n_heads=n_heads, front_pad=front_pad)
    return pl.pallas_call(
        kern,
        grid=(b, nc),
        in_specs=[blk(conv_dim), blk(d_ssm), blk(2 * dc), blk(LANES),
                  full(xbcm), full(zm), full(glum), full(dtm)] + [full(p) for p in params],
        out_specs=[blk(d_ssm), blk(dc),
                   pl.BlockSpec((1, n_heads, SSD_HEAD_DIM, SSD_STATE), lambda i, c: (i, 0, 0, 0)),
                   pl.BlockSpec((1, SSD_CONV - 1, conv_dim), lambda i, c: (i, 0, 0)),
                   pl.BlockSpec((1, CONF_WIDTH - 1, dc), lambda i, c: (i, 0, 0))],
        out_shape=[jax.ShapeDtypeStruct((b, seq, d_ssm), BF16),
                   jax.ShapeDtypeStruct((b, seq, dc), BF16),
                   jax.ShapeDtypeStruct((b, n_heads, SSD_HEAD_DIM, SSD_STATE), F32),
                   jax.ShapeDtypeStruct((b, SSD_CONV - 1, conv_dim), F32),
                   jax.ShapeDtypeStruct((b, CONF_WIDTH - 1, dc), F32)],
        scratch_shapes=[pltpu.VMEM((SSD_STATE, d_ssm), F32),
                        pltpu.VMEM((SUBLANES + CHUNK, conv_dim), F32),
                        pltpu.VMEM((32 + CHUNK, dc), F32)],
        compiler_params=_cparams(("parallel", "arbitrary")),
        name="seq",
    )(xbc, z, glu, dt, xbcm, zm, glum, dtm, *params)


def _sample_kernel(xbc_ref, z_ref, glu_ref, dt_ref, ssm_ref, shist_ref, chist_ref,
                   cw_ref, cb_ref, dtb_ref, alog_ref, dskip_ref, gn_ref,
                   ccw_ref, ccb_ref, lng_ref, lnb_ref,
                   y_ref, c_ref, ssm_o, shist_o, chist_o, *, d_ssm, n_heads):
    sb = xbc_ref.shape[0]
    n = SSD_STATE
    gw = SSD_GROUPS * n
    hp = d_ssm // SSD_GROUPS
    heads_per_group = n_heads // SSD_GROUPS

    x_new = xbc_ref[...].astype(F32)
    acc = cb_ref[...] + cw_ref[SSD_CONV - 1:SSD_CONV, :] * x_new
    for k in range(SSD_CONV - 1):
        acc = acc + cw_ref[k:k + 1, :] * shist_ref[:, k, :]
    for k in range(SSD_CONV - 2):
        shist_o[:, k, :] = shist_ref[:, k + 1, :]
    shist_o[:, SSD_CONV - 2, :] = x_new
    xact = _silu(acc)
    xs = xact[:, :d_ssm]
    dtv = _softplus(dt_ref[...] + dtb_ref[...])
    a = -jnp.exp(alog_ref[...])
    decay = jnp.exp(dtv * a)

    hsel = (lax.broadcasted_iota(jnp.int32, (LANES, d_ssm), 1) // SSD_HEAD_DIM
            == lax.broadcasted_iota(jnp.int32, (LANES, d_ssm), 0)).astype(F32)
    dt_x = jnp.dot(dtv, hsel, preferred_element_type=F32, precision=HIGHEST)
    xdt = (xs * dt_x).astype(BF16)
    rowid = lax.broadcasted_iota(jnp.int32, (sb, 1), 0)

    y_rows = []
    for i in range(sb):
        sel = rowid == i
        xi = jnp.where(sel, xdt, jnp.zeros_like(xdt))
        parts = []
        for g in range(SSD_GROUPS):
            bm = xact[:, d_ssm + g * n:d_ssm + (g + 1) * n].astype(BF16)
            cmat = xact[:, d_ssm + gw + g * n:d_ssm + gw + (g + 1) * n].astype(BF16)
            outer = lax.dot_general(xi[:, g * hp:(g + 1) * hp], bm, (((0,), (0,)), ((), ())),
                                    preferred_element_type=F32)
            outer = outer.reshape(heads_per_group, SSD_HEAD_DIM, n)
            hs = slice(g * heads_per_group, (g + 1) * heads_per_group)
            s_old = ssm_ref[i, hs]
            dec_i = jnp.stack([jnp.broadcast_to(decay[i:i + 1, h:h + 1], (SSD_HEAD_DIM, n))
                               for h in range(g * heads_per_group, (g + 1) * heads_per_group)])
            s_new = s_old * dec_i + outer
            ssm_o[i, hs] = s_new
            yv = lax.dot_general(cmat, s_new.reshape(hp, n).astype(BF16), (((1,), (1,)), ((), ())),
                                 preferred_element_type=F32)
            parts.append(yv)
        yfull = jnp.concatenate(parts, axis=1)
        y_rows.append(jnp.where(sel, yfull, 0.0))
    y = y_rows[0]
    for r in y_rows[1:]:
        y = y + r
    y = y + dskip_ref[...] * xs
    y = y * _silu(z_ref[...].astype(F32))
    gsz = d_ssm // SSD_GROUPS
    y = jnp.concatenate(
        [_rms(y[:, g * gsz:(g + 1) * gsz], gn_ref[:, g * gsz:(g + 1) * gsz]) for g in range(SSD_GROUPS)],
        axis=1)
    y_ref[...] = y.astype(BF16)

    glu = glu_ref[...].astype(F32)
    dc = glu.shape[1] // 2
    cval = glu[:, :dc] * _sigmoid(glu[:, dc:])
    hist = CONF_WIDTH - 1
    cacc = ccb_ref[...] + ccw_ref[hist:hist + 1, :] * cval
    cacc = cacc + jnp.sum(chist_ref[...] * ccw_ref[0:hist, :][None], axis=1)
    chist_o[:, 0:hist - 1, :] = chist_ref[:, 1:hist, :]
    chist_o[:, hist - 1, :] = cval
    mu = jnp.mean(cacc, axis=-1, keepdims=True)
    var = jnp.mean(jnp.square(cacc - mu), axis=-1, keepdims=True)
    cn = (cacc - mu) * lax.rsqrt(var + EPS) * lng_ref[...] + lnb_ref[...]
    c_ref[...] = _silu(cn).astype(BF16)


def _sample(xbc, z, glu, dt, ssm, shist, chist, params, *, n_heads):
    bs, conv_dim = xbc.shape
    d_ssm = z.shape[1]
    dc = glu.shape[1] // 2
    sb = SUBLANES
    assert bs % sb == 0
    row = lambda n: pl.BlockSpec((sb, n), lambda i: (i, 0))
    full = lambda a: pl.BlockSpec(a.shape, lambda i: (0,) * a.ndim)
    st_spec = pl.BlockSpec((sb, n_heads, SSD_HEAD_DIM, SSD_STATE), lambda i: (i, 0, 0, 0))
    sh_spec = pl.BlockSpec((sb, SSD_CONV - 1, conv_dim), lambda i: (i, 0, 0))
    ch_spec = pl.BlockSpec((sb, CONF_WIDTH - 1, dc), lambda i: (i, 0, 0))
    kern = functools.partial(_sample_kernel, d_ssm=d_ssm, n_heads=n_heads)
    return pl.pallas_call(
        kern,
        grid=(bs // sb,),
        in_specs=[row(conv_dim), row(d_ssm), row(2 * dc), row(LANES), st_spec, sh_spec, ch_spec]
        + [full(p) for p in params],
        out_specs=[row(d_ssm), row(dc), st_spec, sh_spec, ch_spec],
        out_shape=[jax.ShapeDtypeStruct((bs, d_ssm), BF16),
                   jax.ShapeDtypeStruct((bs, dc), BF16),
                   jax.ShapeDtypeStruct(ssm.shape, F32),
                   jax.ShapeDtypeStruct(shist.shape, F32),
                   jax.ShapeDtypeStruct(chist.shape, F32)],
        compiler_params=_cparams(("parallel",)),
        name="sample_step",
    )(xbc, z, glu, dt, ssm, shist, chist, *params)


def _outproj_kernel(y_ref, c_ref, res_ref, wy_ref, wc_ref, g_ref, wr_ref, br_ref, cnt_in_ref, *rest):
    h_ref, xn_ref, idx_ref, gate_ref, rank_ref, cnt_ref, carry_ref = rest[-7:]
    i = pl.program_id(0)

    @pl.when(i == 0)
    def _():
        carry_ref[...] = cnt_in_ref[...]

    h = (res_ref[...]
         + jnp.dot(y_ref[...], wy_ref[...], preferred_element_type=F32)
         + jnp.dot(c_ref[...], wc_ref[...], preferred_element_type=F32))
    h_ref[...] = h
    xn = _rms(h, g_ref[...])
    _rows_to_tiles(xn_ref, xn)
    tm = h.shape[0]
    logits = lax.dot_general(wr_ref[...], xn, (((1,), (1,)), ((), ())),
                             preferred_element_type=F32, precision=HIGHEST) + br_ref[...]
    eid = lax.broadcasted_iota(jnp.int32, logits.shape, 0)
    vals, idxs, hots = [], [], []
    work = logits
    for _ in range(TOP_K):
        m = jnp.max(work, axis=0, keepdims=True)
        sel = jnp.min(jnp.where(work == m, eid, N_EXPERTS), axis=0, keepdims=True)
        hot = eid == sel
        work = jnp.where(hot, -jnp.inf, work)
        vals.append(m)
        idxs.append(sel)
        hots.append(hot)
    exps = [jnp.exp(v - vals[0]) for v in vals]
    den = exps[0]
    for e in exps[1:]:
        den = den + e
    chosen = hots[0]
    for hot in hots[1:]:
        chosen = jnp.logical_or(chosen, hot)
    chosen_f = chosen.astype(F32)
    r = lax.broadcasted_iota(jnp.int32, (tm, tm), 0)
    cidx = lax.broadcasted_iota(jnp.int32, (tm, tm), 1)
    upper = (r < cidx).astype(BF16)
    cum = jnp.dot(chosen_f.astype(BF16), upper, preferred_element_type=F32) + carry_ref[:, 0:1]
    for k in range(TOP_K):
        idx_ref[k:k + 1, :] = idxs[k]
        gate_ref[k:k + 1, :] = exps[k] / den
        rank_ref[k:k + 1, :] = jnp.sum(jnp.where(hots[k], cum, 0.0), axis=0, keepdims=True).astype(jnp.int32)
    carry_ref[...] = carry_ref[...] + jnp.sum(chosen_f, axis=1, keepdims=True)
    cnt_ref[...] = carry_ref[...]


def _outproj(y, c, res, wy, wc, g, wr_t, br, cnt_in, *, xn_rows, xn_buf=None, row_off=0):
    t, d = res.shape
    assert d == SUBLANES * LANES
    tm = min(512, t)
    assert t % tm == 0 and row_off % tm == 0
    off = row_off // tm
    row = lambda n: pl.BlockSpec((tm, n), lambda i: (i, 0))
    col = pl.BlockSpec((TOP_K, tm), lambda i: (0, i))
    full = lambda a: pl.BlockSpec(a.shape, lambda i: (0,) * a.ndim)
    in_specs = [row(y.shape[1]), row(c.shape[1]), row(d), full(wy), full(wc), full(g), full(wr_t),
                full(br), full(cnt_in)]
    args = [y, c, res, wy, wc, g, wr_t, br, cnt_in]
    aliases = {}
    if xn_buf is not None:
        in_specs.append(pl.BlockSpec(memory_space=pl.ANY))
        args.append(xn_buf)
        aliases = {len(args) - 1: 1}
    return pl.pallas_call(
        _outproj_kernel,
        grid=(t // tm,),
        in_specs=in_specs,
        out_specs=[row(d), pl.BlockSpec((tm * SUBLANES, LANES), lambda i: (i + off, 0)), col, col, col,
                   full(cnt_in)],
        out_shape=[jax.ShapeDtypeStruct((t, d), F32),
                   jax.ShapeDtypeStruct((xn_rows * SUBLANES, LANES), F32),
                   jax.ShapeDtypeStruct((TOP_K, t), jnp.int32),
                   jax.ShapeDtypeStruct((TOP_K, t), F32),
                   jax.ShapeDtypeStruct((TOP_K, t), jnp.int32),
                   jax.ShapeDtypeStruct(cnt_in.shape, F32)],
        scratch_shapes=[pltpu.VMEM(cnt_in.shape, F32)],
        input_output_aliases=aliases,
        compiler_params=_cparams(("arbitrary",)),
        name="outproj_router",
    )(*args)


_INV_GROUP = 16


def _inverse_kernel(dest_ref, inv_ref, *, pad_base, init_groups, fill_groups):
    step = pl.program_id(0)
    half = pl.num_programs(0) // 2
    groups_per_row = LANES // _INV_GROUP

    def split(j):
        r = jnp.right_shift(j, groups_per_row.bit_length() - 1)
        return r, jnp.bitwise_and(j, groups_per_row - 1) * _INV_GROUP

    @pl.when(step < half)
    def _():
        def init(jj, carry):
            r, c0 = split(step * init_groups + jj)
            base = pad_base + jnp.bitwise_and(r, MOE_BLOCK // LANES - 1) * LANES + c0
            for c in range(_INV_GROUP):
                inv_ref[r, c0 + c] = (base + c) * TOP_K
            return carry

        lax.fori_loop(0, init_groups, init, 0)

    @pl.when(step >= half)
    def _():
        def fill(jj, carry):
            j = (step - half) * fill_groups + jj
            r, c0 = split(j)
            for c in range(_INV_GROUP):
                d = dest_ref[r, c0 + c]
                inv_ref[jnp.right_shift(d, LANE_BITS), jnp.bitwise_and(d, LANES - 1)] = j * _INV_GROUP + c
            return carry

        lax.fori_loop(0, fill_groups, fill, 0)


def _build_inverse(dest2d, n_rows, pad_base):
    assert n_rows % LANES == 0
    n_init = n_rows // _INV_GROUP
    n_fill = dest2d.shape[0] * LANES // _INV_GROUP
    slices = max(g for g in range(1, 17) if n_init % g == 0 and n_fill % g == 0)
    kern = functools.partial(_inverse_kernel, pad_base=pad_base, init_groups=n_init // slices,
                             fill_groups=n_fill // slices)
    return pl.pallas_call(
        kern,
        grid=(2 * slices,),
        in_specs=[pl.BlockSpec(memory_space=pltpu.SMEM)],
        out_specs=pl.BlockSpec(memory_space=pltpu.SMEM),
        out_shape=jax.ShapeDtypeStruct((n_rows // LANES, LANES), jnp.int32),
        compiler_params=_cparams(("arbitrary",)),
        name="inverse_table",
    )(dest2d)


def _moe_kernel(be_ref, na_ref, tok_ref, row_ref, xn_hbm, wg_ref, bg_ref, wu_ref, bu_ref, wd_ref, bd_ref,
                yk_hbm, wg_s, wu_s, wd_s, x0, x1, o0, o1, gsem, ssem, *, trash_row):
    i = pl.program_id(0)
    n_active = na_ref[0]
    last_block = pl.num_programs(0) - 2
    xbuf, obuf = (x0, x1), (o0, o1)
    tab_rows = MOE_BLOCK // LANES
    prev = be_ref[jnp.maximum(i - 1, 0)]
    new_expert = jnp.logical_or(i == 0, be_ref[i] != prev)
    active = i < n_active

    def gather_start(block, slot):
        for r in range(MOE_BLOCK):
            src = pl.multiple_of(tok_ref[block * tab_rows + r // LANES, r % LANES], SUBLANES)
            pltpu.make_async_copy(xn_hbm.at[pl.ds(src, SUBLANES), :],
                                  xbuf[slot].at[pl.ds(r * SUBLANES, SUBLANES), :], gsem.at[slot]).start()

    def gather_wait(slot):
        pltpu.make_async_copy(xn_hbm.at[pl.ds(0, MOE_BLOCK * SUBLANES), :], xbuf[slot],
                              gsem.at[slot]).wait()

    def scatter_start(block, slot, real):
        for r in range(MOE_BLOCK):
            dst = jnp.where(real, row_ref[block * tab_rows + r // LANES, r % LANES],
                            (trash_row + r) * SUBLANES)
            pltpu.make_async_copy(obuf[slot].at[pl.ds(r * SUBLANES, SUBLANES), :],
                                  yk_hbm.at[pl.ds(pl.multiple_of(dst, SUBLANES), SUBLANES), :],
                                  ssem.at[slot]).start()

    def scatter_wait(slot):
        pltpu.make_async_copy(obuf[slot], yk_hbm.at[pl.ds(0, MOE_BLOCK * SUBLANES), :],
                              ssem.at[slot]).wait()

    @pl.when(i == 0)
    def _():
        o0[...] = jnp.zeros_like(o0)
        o1[...] = jnp.zeros_like(o1)
        gather_start(0, 0)

    @pl.when(jnp.logical_and(active, new_expert))
    def _():
        wg_s[...] = wg_ref[0].astype(BF16)
        wu_s[...] = wu_ref[0].astype(BF16)
        wd_s[...] = wd_ref[0].astype(BF16)

    for p in (0, 1):
        mine = (i % 2) == p

        @pl.when(jnp.logical_and(active, mine))
        def _():
            gather_wait(p)
            gather_start(jnp.minimum(i + 1, last_block), 1 - p)
            scatter_start(jnp.maximum(i - 1, 0), 1 - p, i > 0)
            x = _tiles_to_rows(xbuf[p], MOE_BLOCK).astype(BF16)
            gt = jnp.minimum(jnp.dot(x, wg_s[...], preferred_element_type=F32) + bg_ref[0], SWIGLU_LIMIT)
            up = jnp.clip(jnp.dot(x, wu_s[...], preferred_element_type=F32) + bu_ref[0],
                          -SWIGLU_LIMIT, SWIGLU_LIMIT)
            hdn = (up + 1.0) * gt * _sigmoid(gt * SWIGLU_ALPHA)
            y = jnp.dot(hdn.astype(BF16), wd_s[...], preferred_element_type=F32) + bd_ref[0]
            _rows_to_tiles(obuf[p], y)
            scatter_wait(1 - p)

        @pl.when(jnp.logical_and(i == n_active, mine))
        def _():
            gather_wait(p)
            scatter_start(i - 1, 1 - p, True)
            scatter_wait(1 - p)


def _moe(block_expert, n_active, tok2d, row2d, xn, w_gate, b_gate, w_up, b_up, w_down, b_down, *,
         yk_rows, trash_row):
    d = w_gate.shape[1]
    assert d == SUBLANES * LANES and xn.shape[1] == LANES
    nb = tok2d.shape[0] * LANES // MOE_BLOCK
    dff = w_gate.shape[2]
    wspec = lambda a: pl.BlockSpec((1,) + a.shape[1:], lambda i, be, na, tk, rw: (be[i], 0, 0))
    blk = pltpu.VMEM((MOE_BLOCK * SUBLANES, LANES), F32)
    kern = functools.partial(_moe_kernel, trash_row=trash_row)
    return pl.pallas_call(
        kern,
        grid_spec=pltpu.PrefetchScalarGridSpec(
            num_scalar_prefetch=4,
            grid=(nb + 1,),
            in_specs=[pl.BlockSpec(memory_space=pl.ANY), wspec(w_gate), wspec(b_gate), wspec(w_up),
                      wspec(b_up), wspec(w_down), wspec(b_down)],
            out_specs=pl.BlockSpec(memory_space=pl.ANY),
            scratch_shapes=[pltpu.VMEM((d, dff), BF16), pltpu.VMEM((d, dff), BF16),
                            pltpu.VMEM((dff, d), BF16), blk, blk, blk, blk,
                            pltpu.SemaphoreType.DMA((2,)), pltpu.SemaphoreType.DMA((2,))],
        ),
        out_shape=jax.ShapeDtypeStruct((yk_rows * SUBLANES, LANES), F32),
        compiler_params=_cparams(("arbitrary",)),
        name="moe_experts",
    )(block_expert, n_active, tok2d, row2d, xn, w_gate, b_gate, w_up, b_up, w_down, b_down)


def _combine_kernel(h_ref, gate_ref, gf_ref, *rest):
    yk_refs, o_ref = rest[:TOP_K], rest[TOP_K]
    acc = h_ref[...]
    tm = acc.shape[0]
    for k in range(TOP_K):
        acc = acc + gate_ref[:, k:k + 1] * _tiles_to_rows(yk_refs[k], tm)
    o_ref[...] = _rms(acc, gf_ref[...])


def _combine(h, gates, g_final, yk, row_off, plane_tokens):
    t, d = h.shape
    tm = min(512, t)
    assert t % tm == 0
    plane = lambda k: pl.BlockSpec(
        (pl.Element(tm * SUBLANES), pl.Element(LANES)),
        lambda i: ((k * plane_tokens + row_off + i * tm) * SUBLANES, 0))
    return pl.pallas_call(
        _combine_kernel,
        grid=(t // tm,),
        in_specs=[pl.BlockSpec((tm, d), lambda i: (i, 0)),
                  pl.BlockSpec((tm, TOP_K), lambda i: (i, 0)),
                  pl.BlockSpec(g_final.shape, lambda i: (0, 0))] + [plane(k) for k in range(TOP_K)],
        out_specs=pl.BlockSpec((tm, d), lambda i: (i, 0)),
        out_shape=jax.ShapeDtypeStruct((t, d), F32),
        compiler_params=_cparams(("parallel",)),
        name="combine",
    )(h, gates, g_final, *([yk] * TOP_K))


def _pad_lanes(v, fill=0.0):
    v = v.reshape(1, -1).astype(F32)
    return jnp.pad(v, ((0, 0), (0, LANES - v.shape[1])), constant_values=fill)


def kernel(x_prompt, x_sample, state_ssm, state_ssd_conv, state_conf_conv, meta_tokens, g_mix, w_in,
           conv_ssd_w, conv_ssd_b, dt_bias, a_log, d_skip, g_ssd_norm, conv_conf_w, conv_conf_b,
           ln_conf_g, ln_conf_b, w_out, g_ffn, w_router, b_router, w_gate, b_gate, w_up, b_up,
           w_down, b_down, g_final):
    bp, seq, d = x_prompt.shape
    bs = x_sample.shape[0]
    depth = w_in.shape[0]
    assert depth == 1 and x_sample.shape[1] == 1 and seq % CHUNK == 0
    n_heads = a_log.shape[1]
    d_ssm = n_heads * SSD_HEAD_DIM
    conv_dim = conv_ssd_w.shape[2]
    dc = conv_conf_w.shape[2]
    l = 0

    o1, o2, o3 = d_ssm, d_ssm + conv_dim, d_ssm + conv_dim + n_heads
    w_in_l = w_in[l]
    wz = w_in_l[:, :o1].astype(BF16)
    wx = w_in_l[:, o1:o2].astype(BF16)
    wdt = jnp.pad(w_in_l[:, o2:o3], ((0, 0), (0, LANES - n_heads))).astype(BF16)
    wglu = w_in_l[:, o3:].astype(BF16)
    row2 = lambda v: v.reshape(1, -1).astype(F32)
    seq_params = (
        conv_ssd_w[l], row2(conv_ssd_b[l]), _pad_lanes(dt_bias[l]), _pad_lanes(a_log[l]),
        row2(jnp.repeat(d_skip[l], SSD_HEAD_DIM)), row2(g_ssd_norm[l]),
        jnp.pad(conv_conf_w[l], ((0, 32 - CONF_WIDTH), (0, 0))), row2(conv_conf_b[l]),
        row2(ln_conf_g[l]), row2(ln_conf_b[l]))
    g_mix_r = row2(g_mix[l])

    xp2 = x_prompt.reshape(bp * seq, d)
    front_pad = (-N_META) % CHUNK
    small = jnp.concatenate([jnp.zeros((front_pad, d), F32), meta_tokens.astype(F32),
                             x_sample.reshape(bs, d)], axis=0)
    z_p, xbc_p, glu_p, dt_p = _inproj(xp2, g_mix_r, wz, wx, wglu, wdt)
    z_m, xbc_m, glu_m, dt_m = _inproj(small, g_mix_r, wz, wx, wglu, wdt)

    r3 = lambda a: a.reshape(bp, seq, a.shape[-1])
    y_p, c_p, ssm_p, shist_p, chist_p = _seq(
        r3(xbc_p), r3(z_p), r3(glu_p), r3(dt_p),
        xbc_m[:CHUNK], z_m[:CHUNK], glu_m[:CHUNK], dt_m[:CHUNK], seq_params, n_heads=n_heads)

    y_s, c_s, ssm_s, shist_s, chist_s = _sample(
        xbc_m[CHUNK:], z_m[CHUNK:], glu_m[CHUNK:], dt_m[CHUNK:],
        state_ssm[l], state_ssd_conv[l], state_conf_conv[l], seq_params, n_heads=n_heads)

    w_out_l = w_out[l]
    wy = w_out_l[:d_ssm].astype(BF16)
    wc = w_out_l[d_ssm:].astype(BF16)
    g_ffn_r = row2(g_ffn[l])
    wr_t = w_router[l].T.astype(F32)
    br = b_router[l].reshape(N_EXPERTS, 1).astype(F32)
    cnt0 = jnp.zeros((N_EXPERTS, LANES), F32)
    t_p = bp * seq
    t_all = t_p + bs
    assert (t_all * TOP_K) % LANES == 0
    xn_all = jnp.zeros((t_all * SUBLANES, LANES), F32)
    h_p, xn_all, idx_p, gate_p, rank_p, cnt1 = _outproj(
        y_p.reshape(t_p, d_ssm), c_p.reshape(t_p, dc), xp2, wy, wc, g_ffn_r, wr_t, br, cnt0,
        xn_rows=t_all, xn_buf=xn_all)
    h_s, xn_all, idx_s, gate_s, rank_s, cnt2 = _outproj(
        y_s, c_s, x_sample.reshape(bs, d), wy, wc, g_ffn_r, wr_t, br, cnt1,
        xn_rows=t_all, xn_buf=xn_all, row_off=t_p)

    counts = cnt2[:, 0].astype(jnp.int32)
    padded = (counts + MOE_BLOCK - 1) // MOE_BLOCK * MOE_BLOCK
    pad_end = jnp.cumsum(padded)
    pad_start = pad_end - padded
    n_blocks = -(-(t_all * TOP_K) // MOE_BLOCK) + N_EXPERTS
    rows = n_blocks * MOE_BLOCK
    blk_first = jnp.arange(n_blocks + 1, dtype=jnp.int32) * MOE_BLOCK
    block_expert = jnp.minimum(jnp.sum(pad_end[None, :] <= blk_first[:, None], axis=1),
                               N_EXPERTS - 1).astype(jnp.int32)
    n_active = (pad_end[-1] // MOE_BLOCK).astype(jnp.int32).reshape(1)
    idx_all = jnp.concatenate([idx_p, idx_s], axis=1)
    rank_all = jnp.concatenate([rank_p, rank_s], axis=1)
    experts = jnp.arange(N_EXPERTS, dtype=jnp.int32)[:, None, None]
    start_of = jnp.sum(jnp.where(idx_all[None] == experts, pad_start[:, None, None], 0), axis=0)
    dest = (start_of + rank_all).T.reshape(-1, LANES).astype(jnp.int32)

    trash = TOP_K * t_all
    inv = _build_inverse(dest, rows, t_all)
    inv_tok = inv // TOP_K
    tok2d = jnp.minimum(inv_tok, t_all - 1) * SUBLANES
    row2d = jnp.where(inv_tok < t_all, (inv % TOP_K) * t_all + inv_tok, trash + inv_tok - t_all) * SUBLANES
    yk = _moe(block_expert, n_active, tok2d, row2d, xn_all,
              w_gate[l], b_gate[l].reshape(N_EXPERTS, 1, -1), w_up[l], b_up[l].reshape(N_EXPERTS, 1, -1),
              w_down[l], b_down[l].reshape(N_EXPERTS, 1, -1), yk_rows=trash + MOE_BLOCK, trash_row=trash)
    g_fin = row2(g_final)
    y_prompt = _combine(h_p, gate_p.T, g_fin, yk, 0, t_all).reshape(bp, seq, d)
    y_sample = _combine(h_s, gate_s.T, g_fin, yk, t_p, t_all).reshape(bs, 1, d)

    return (y_prompt, y_sample, ssm_p[None], shist_p[None], chist_p[None],
            ssm_s[None], shist_s[None], chist_s[None])
```

```python
import functools

import jax
import jax.numpy as jnp
from jax import lax
from jax.experimental import pallas as pl
from jax.experimental.pallas import tpu as pltpu

F32 = jnp.float32
BF16 = jnp.bfloat16
HIGHEST = lax.Precision.HIGHEST

N_META = 16
SSD_HEAD_DIM = 64
SSD_GROUPS = 2
SSD_STATE = 128
SSD_CONV = 4
CHUNK = 128
CONF_WIDTH = 31
N_EXPERTS = 32
TOP_K = 4
SWIGLU_LIMIT = 7.0
SWIGLU_ALPHA = 1.702
EPS = 1e-5

LANES = 128
LANE_BITS = LANES.bit_length() - 1
SUBLANES = 8
MOE_BLOCK = 256
VMEM_LIMIT = 56 * 1024 * 1024


def _cparams(sem):
    return pltpu.CompilerParams(dimension_semantics=sem, vmem_limit_bytes=VMEM_LIMIT)


def _sigmoid(x):
    return 1.0 / (1.0 + jnp.exp(-x))


def _tiles_to_rows(ref, n):
    return jnp.concatenate([ref[pl.ds(s, n, stride=SUBLANES), :] for s in range(SUBLANES)], axis=1)


def _rows_to_tiles(ref, val):
    n = val.shape[0]
    for s in range(SUBLANES):
        ref[pl.ds(s, n, stride=SUBLANES), :] = val[:, s * LANES:(s + 1) * LANES]


def _silu(x):
    return x * _sigmoid(x)


def _softplus(x):
    return jnp.maximum(x, 0.0) + jnp.log1p(jnp.exp(-jnp.abs(x)))


def _rms(x, g):
    return x * lax.rsqrt(jnp.mean(x * x, axis=-1, keepdims=True) + EPS) * g


def _inproj_kernel(x_ref, g_ref, wz_ref, wx_ref, wg_ref, wd_ref, z_ref, xbc_ref, glu_ref, dt_ref):
    u = _rms(x_ref[...], g_ref[...]).astype(BF16)
    z_ref[...] = jnp.dot(u, wz_ref[...], preferred_element_type=F32).astype(BF16)
    xbc_ref[...] = jnp.dot(u, wx_ref[...], preferred_element_type=F32).astype(BF16)
    glu_ref[...] = jnp.dot(u, wg_ref[...], preferred_element_type=F32).astype(BF16)
    dt_ref[...] = jnp.dot(u, wd_ref[...], preferred_element_type=F32)


def _inproj(x, g, wz, wx, wg, wd):
    t, d = x.shape
    tm = min(512, t)
    assert t % tm == 0
    row = lambda n: pl.BlockSpec((tm, n), lambda i: (i, 0))
    full = lambda a: pl.BlockSpec(a.shape, lambda i: (0, 0))
    return pl.pallas_call(
        _inproj_kernel,
        grid=(t // tm,),
        in_specs=[row(d), full(g), full(wz), full(wx), full(wg), full(wd)],
        out_specs=[row(wz.shape[1]), row(wx.shape[1]), row(wg.shape[1]), row(wd.shape[1])],
        out_shape=[jax.ShapeDtypeStruct((t, wz.shape[1]), BF16),
                   jax.ShapeDtypeStruct((t, wx.shape[1]), BF16),
                   jax.ShapeDtypeStruct((t, wg.shape[1]), BF16),
                   jax.ShapeDtypeStruct((t, wd.shape[1]), F32)],
        compiler_params=_cparams(("parallel",)),
        name="inproj",
    )(x, g, wz, wx, wg, wd)


def _seq_kernel(xbc_ref, z_ref, glu_ref, dt_ref, xbcm_ref, zm_ref, glum_ref, dtm_ref,
                cw_ref, cb_ref, dtb_ref, alog_ref, dskip_ref, gn_ref,
                ccw_ref, ccb_ref, lng_ref, lnb_ref,
                y_ref, c_ref, ssm_ref, shist_ref, chist_ref,
                st_ref, xbuf_ref, cbuf_ref, *, d_ssm, n_heads, front_pad):
    c = pl.program_id(1)
    last = pl.num_programs(1) - 1
    is_meta = c == 0
    q = CHUNK
    n = SSD_STATE
    gw = SSD_GROUPS * n

    @pl.when(is_meta)
    def _():
        st_ref[...] = jnp.zeros_like(st_ref)
        xbuf_ref[...] = jnp.zeros_like(xbuf_ref)
        cbuf_ref[...] = jnp.zeros_like(cbuf_ref)

    xbc_raw = jnp.where(is_meta, xbcm_ref[...], xbc_ref[0]).astype(F32)
    xbuf_ref[SUBLANES:SUBLANES + q, :] = xbc_raw
    acc = cb_ref[...] + cw_ref[SSD_CONV - 1:SSD_CONV, :] * xbc_raw
    for j in range(1, SSD_CONV):
        acc = acc + cw_ref[SSD_CONV - 1 - j:SSD_CONV - j, :] * xbuf_ref[SUBLANES - j:SUBLANES - j + q, :]
    xbuf_ref[0:SUBLANES, :] = xbuf_ref[q:q + SUBLANES, :]
    xact = _silu(acc)
    xs = xact[:, :d_ssm]
    bm = [xact[:, d_ssm + g * n:d_ssm + (g + 1) * n] for g in range(SSD_GROUPS)]
    cm = [xact[:, d_ssm + gw + g * n:d_ssm + gw + (g + 1) * n] for g in range(SSD_GROUPS)]

    dt_raw = jnp.where(is_meta, dtm_ref[...], dt_ref[0])
    rows = lax.broadcasted_iota(jnp.int32, (q, LANES), 0)
    cols = lax.broadcasted_iota(jnp.int32, (q, LANES), 1)
    valid = jnp.logical_or(jnp.logical_not(is_meta), rows >= front_pad)
    dtv = jnp.where(valid, _softplus(dt_raw + dtb_ref[...]), 0.0)
    da = dtv * (-jnp.exp(alog_ref[...]))
    causal = rows >= cols
    tri = causal.astype(F32)
    a_cum = jnp.dot(tri, da, preferred_element_type=F32, precision=HIGHEST)
    a_cum_t = a_cum.T
    dt_t = dtv.T
    w_t = jnp.exp(a_cum_t[:, q - 1:q] - a_cum_t) * dt_t
    a_last = a_cum[q - 1:q, :]

    bm_t = [b.T for b in bm]
    cb = [lax.dot_general(cm[g].astype(BF16), bm[g].astype(BF16), (((1,), (1,)), ((), ())),
                          preferred_element_type=F32) for g in range(SSD_GROUPS)]
    lane = lax.broadcasted_iota(jnp.int32, (q, LANES), 1)
    lo = lane < SSD_HEAD_DIM
    heads_per_group = n_heads // SSD_GROUPS

    y_parts = []
    for j in range(n_heads // 2):
        g = (2 * j) // heads_per_group
        m_l, ce_l, bw_l = [], [], []
        for h in (2 * j, 2 * j + 1):
            col = a_cum[:, h:h + 1]
            seg = col - a_cum_t[h:h + 1, :]
            dec = jnp.exp(jnp.where(causal, seg, -jnp.inf))
            m_l.append((cb[g] * dec * dt_t[h:h + 1, :]).astype(BF16))
            ce_l.append((cm[g] * jnp.exp(col)).astype(BF16))
            bw_l.append((bm_t[g] * w_t[h:h + 1, :]).astype(BF16))
        sl = slice(j * LANES, (j + 1) * LANES)
        xs_pair = xs[:, sl]
        rhs_x = jnp.concatenate([jnp.where(lo, xs_pair, 0.0), jnp.where(lo, 0.0, xs_pair)],
                                axis=0).astype(BF16)
        st_pair = st_ref[:, sl]
        rhs_s = jnp.concatenate([jnp.where(lo, st_pair, 0.0), jnp.where(lo, 0.0, st_pair)],
                                axis=0).astype(BF16)
        lhs_y = jnp.concatenate(m_l + ce_l, axis=1)
        y_pair = jnp.dot(lhs_y, jnp.concatenate([rhs_x, rhs_s], axis=0), preferred_element_type=F32)
        st_new = jnp.dot(jnp.concatenate(bw_l, axis=1), rhs_x, preferred_element_type=F32)
        cd = jnp.exp(jnp.where(lo[0:1, :], a_last[:, 2 * j:2 * j + 1], a_last[:, 2 * j + 1:2 * j + 2]))
        st_ref[:, sl] = st_pair * cd + st_new
        y_parts.append(y_pair + dskip_ref[:, sl] * xs_pair)

    y = jnp.concatenate(y_parts, axis=1)
    zf = jnp.where(is_meta, zm_ref[...], z_ref[0]).astype(F32)
    y = y * _silu(zf)
    gsz = d_ssm // SSD_GROUPS
    y = jnp.concatenate(
        [_rms(y[:, g * gsz:(g + 1) * gsz], gn_ref[:, g * gsz:(g + 1) * gsz]) for g in range(SSD_GROUPS)],
        axis=1)
    y_ref[0] = y.astype(BF16)

    glu = jnp.where(is_meta, glum_ref[...], glu_ref[0]).astype(F32)
    dc = glu.shape[1] // 2
    cval = glu[:, :dc] * _sigmoid(glu[:, dc:])
    hist = CONF_WIDTH - 1
    base = 32
    cbuf_ref[base:base + q, :] = cval
    cacc = ccb_ref[...] + ccw_ref[hist:hist + 1, :] * cval
    for k in range(hist):
        off = base - hist + k
        cacc = cacc + ccw_ref[k:k + 1, :] * cbuf_ref[off:off + q, :]
    cbuf_ref[0:base, :] = cbuf_ref[q:q + base, :]
    mu = jnp.mean(cacc, axis=-1, keepdims=True)
    var = jnp.mean(jnp.square(cacc - mu), axis=-1, keepdims=True)
    cn = (cacc - mu) * lax.rsqrt(var + EPS) * lng_ref[...] + lnb_ref[...]
    c_ref[0] = _silu(cn).astype(BF16)

    @pl.when(c == last)
    def _():
        ssm_ref[0] = st_ref[...].T.reshape(n_heads, SSD_HEAD_DIM, n)
        shist_ref[0] = xbuf_ref[SUBLANES - (SSD_CONV - 1):SUBLANES, :]
        chist_ref[0] = cbuf_ref[base - hist:base, :]


def _seq(xbc, z, glu, dt, xbcm, zm, glum, dtm, params, *, n_heads):
    b, seq, conv_dim = xbc.shape
    d_ssm = z.shape[2]
    dc = glu.shape[2] // 2
    nc = seq // CHUNK + 1
    front_pad = (-N_META) % CHUNK
    blk = lambda n: pl.BlockSpec((1, CHUNK, n), lambda i, c: (i, jnp.maximum(c - 1, 0), 0))
    full = lambda a: pl.BlockSpec(a.shape, lambda i, c: (0,) * a.ndim)
    kern = functools.partial(_seq_kernel, d_ssm=d_ssm, n_heads=n_heads, front_pad=front_pad)
    return pl.pallas_call(
        kern,
        grid=(b, nc),
        in_specs=[blk(conv_dim), blk(d_ssm), blk(2 * dc), blk(LANES),
                  full(xbcm), full(zm), full(glum), full(dtm)] + [full(p) for p in params],
        out_specs=[blk(d_ssm), blk(dc),
                   pl.BlockSpec((1, n_heads, SSD_HEAD_DIM, SSD_STATE), lambda i, c: (i, 0, 0, 0)),
                   pl.BlockSpec((1, SSD_CONV - 1, conv_dim), lambda i, c: (i, 0, 0)),
                   pl.BlockSpec((1, CONF_WIDTH - 1, dc), lambda i, c: (i, 0, 0))],
        out_shape=[jax.ShapeDtypeStruct((b, seq, d_ssm), BF16),
                   jax.ShapeDtypeStruct((b, seq, dc), BF16),
                   jax.ShapeDtypeStruct((b, n_heads, SSD_HEAD_DIM, SSD_STATE), F32),
                   jax.ShapeDtypeStruct((b, SSD_CONV - 1, conv_dim), F32),
                   jax.ShapeDtypeStruct((b, CONF_WIDTH - 1, dc), F32)],
        scratch_shapes=[pltpu.VMEM((SSD_STATE, d_ssm), F32),
                        pltpu.VMEM((SUBLANES + CHUNK, conv_dim), F32),
                        pltpu.VMEM((32 + CHUNK, dc), F32)],
        compiler_params=_cparams(("parallel", "arbitrary")),
        name="seq",
    )(xbc, z, glu, dt, xbcm, zm, glum, dtm, *params)


def _sample_kernel(xbc_ref, z_ref, glu_ref, dt_ref, ssm_ref, shist_ref, chist_ref,
                   cw_ref, cb_ref, dtb_ref, alog_ref, dskip_ref, gn_ref,
                   ccw_ref, ccb_ref, lng_ref, lnb_ref,
                   y_ref, c_ref, ssm_o, shist_o, chist_o, *, d_ssm, n_heads):
    sb = xbc_ref.shape[0]
    n = SSD_STATE
    gw = SSD_GROUPS * n
    hp = d_ssm // SSD_GROUPS
    heads_per_group = n_heads // SSD_GROUPS

    x_new = xbc_ref[...].astype(F32)
    acc = cb_ref[...] + cw_ref[SSD_CONV - 1:SSD_CONV, :] * x_new
    for k in range(SSD_CONV - 1):
        acc = acc + cw_ref[k:k + 1, :] * shist_ref[:, k, :]
    for k in range(SSD_CONV - 2):
        shist_o[:, k, :] = shist_ref[:, k + 1, :]
    shist_o[:, SSD_CONV - 2, :] = x_new
    xact = _silu(acc)
    xs = xact[:, :d_ssm]
    dtv = _softplus(dt_ref[...] + dtb_ref[...])
    a = -jnp.exp(alog_ref[...])
    decay = jnp.exp(dtv * a)

    hsel = (lax.broadcasted_iota(jnp.int32, (LANES, d_ssm), 1) // SSD_HEAD_DIM
            == lax.broadcasted_iota(jnp.int32, (LANES, d_ssm), 0)).astype(F32)
    dt_x = jnp.dot(dtv, hsel, preferred_element_type=F32, precision=HIGHEST)
    xdt = (xs * dt_x).astype(BF16)
    rowid = lax.broadcasted_iota(jnp.int32, (sb, 1), 0)

    y_rows = []
    for i in range(sb):
        sel = rowid == i
        xi = jnp.where(sel, xdt, jnp.zeros_like(xdt))
        parts = []
        for g in range(SSD_GROUPS):
            bm = xact[:, d_ssm + g * n:d_ssm + (g + 1) * n].astype(BF16)
            cmat = xact[:, d_ssm + gw + g * n:d_ssm + gw + (g + 1) * n].astype(BF16)
            outer = lax.dot_general(xi[:, g * hp:(g + 1) * hp], bm, (((0,), (0,)), ((), ())),
                                    preferred_element_type=F32)
            outer = outer.reshape(heads_per_group, SSD_HEAD_DIM, n)
            hs = slice(g * heads_per_group, (g + 1) * heads_per_group)
            s_old = ssm_ref[i, hs]
            dec_i = jnp.stack([jnp.broadcast_to(decay[i:i + 1, h:h + 1], (SSD_HEAD_DIM, n))
                               for h in range(g * heads_per_group, (g + 1) * heads_per_group)])
            s_new = s_old * dec_i + outer
            ssm_o[i, hs] = s_new
            yv = lax.dot_general(cmat, s_new.reshape(hp, n).astype(BF16), (((1,), (1,)), ((), ())),
                                 preferred_element_type=F32)
            parts.append(yv)
        yfull = jnp.concatenate(parts, axis=1)
        y_rows.append(jnp.where(sel, yfull, 0.0))
    y = y_rows[0]
    for r in y_rows[1:]:
        y = y + r
    y = y + dskip_ref[...] * xs
    y = y * _silu(z_ref[...].astype(F32))
    gsz = d_ssm // SSD_GROUPS
    y = jnp.concatenate(
        [_rms(y[:, g * gsz:(g + 1) * gsz], gn_ref[:, g * gsz:(g + 1) * gsz]) for g in range(SSD_GROUPS)],
        axis=1)
    y_ref[...] = y.astype(BF16)

    glu = glu_ref[...].astype(F32)
    dc = glu.shape[1] // 2
    cval = glu[:, :dc] * _sigmoid(glu[:, dc:])
    hist = CONF_WIDTH - 1
    cacc = ccb_ref[...] + ccw_ref[hist:hist + 1, :] * cval
    cacc = cacc + jnp.sum(chist_ref[...] * ccw_ref[0:hist, :][None], axis=1)
    chist_o[:, 0:hist - 1, :] = chist_ref[:, 1:hist, :]
    chist_o[:, hist - 1, :] = cval
    mu = jnp.mean(cacc, axis=-1, keepdims=True)
    var = jnp.mean(jnp.square(cacc - mu), axis=-1, keepdims=True)
    cn = (cacc - mu) * lax.rsqrt(var + EPS) * lng_ref[...] + lnb_ref[...]
    c_ref[...] = _silu(cn).astype(BF16)


def _sample(xbc, z, glu, dt, ssm, shist, chist, params, *, n_heads):
    bs, conv_dim = xbc.shape
    d_ssm = z.shape[1]
    dc = glu.shape[1] // 2
    sb = SUBLANES
    assert bs % sb == 0
    row = lambda n: pl.BlockSpec((sb, n), lambda i: (i, 0))
    full = lambda a: pl.BlockSpec(a.shape, lambda i: (0,) * a.ndim)
    st_spec = pl.BlockSpec((sb, n_heads, SSD_HEAD_DIM, SSD_STATE), lambda i: (i, 0, 0, 0))
    sh_spec = pl.BlockSpec((sb, SSD_CONV - 1, conv_dim), lambda i: (i, 0, 0))
    ch_spec = pl.BlockSpec((sb, CONF_WIDTH - 1, dc), lambda i: (i, 0, 0))
    kern = functools.partial(_sample_kernel, d_ssm=d_ssm, n_heads=n_heads)
    return pl.pallas_call(
        kern,
        grid=(bs // sb,),
        in_specs=[row(conv_dim), row(d_ssm), row(2 * dc), row(LANES), st_spec, sh_spec, ch_spec]
        + [full(p) for p in params],
        out_specs=[row(d_ssm), row(dc), st_spec, sh_spec, ch_spec],
        out_shape=[jax.ShapeDtypeStruct((bs, d_ssm), BF16),
                   jax.ShapeDtypeStruct((bs, dc), BF16),
                   jax.ShapeDtypeStruct(ssm.shape, F32),
                   jax.ShapeDtypeStruct(shist.shape, F32),
                   jax.ShapeDtypeStruct(chist.shape, F32)],
        compiler_params=_cparams(("parallel",)),
        name="sample_step",
    )(xbc, z, glu, dt, ssm, shist, chist, *params)


def _outproj_kernel(y_ref, c_ref, res_ref, wy_ref, wc_ref, g_ref, wr_ref, br_ref, cnt_in_ref, *rest):
    h_ref, xn_ref, idx_ref, gate_ref, rank_ref, cnt_ref, carry_ref = rest[-7:]
    i = pl.program_id(0)

    @pl.when(i == 0)
    def _():
        carry_ref[...] = cnt_in_ref[...]

    h = (res_ref[...]
         + jnp.dot(y_ref[...], wy_ref[...], preferred_element_type=F32)
         + jnp.dot(c_ref[...], wc_ref[...], preferred_element_type=F32))
    h_ref[...] = h
    xn = _rms(h, g_ref[...])
    _rows_to_tiles(xn_ref, xn)
    tm = h.shape[0]
    logits = lax.dot_general(wr_ref[...], xn, (((1,), (1,)), ((), ())),
                             preferred_element_type=F32, precision=HIGHEST) + br_ref[...]
    eid = lax.broadcasted_iota(jnp.int32, logits.shape, 0)
    vals, idxs, hots = [], [], []
    work = logits
    for _ in range(TOP_K):
        m = jnp.max(work, axis=0, keepdims=True)
        sel = jnp.min(jnp.where(work == m, eid, N_EXPERTS), axis=0, keepdims=True)
        hot = eid == sel
        work = jnp.where(hot, -jnp.inf, work)
        vals.append(m)
        idxs.append(sel)
        hots.append(hot)
    exps = [jnp.exp(v - vals[0]) for v in vals]
    den = exps[0]
    for e in exps[1:]:
        den = den + e
    chosen = hots[0]
    for hot in hots[1:]:
        chosen = jnp.logical_or(chosen, hot)
    chosen_f = chosen.astype(F32)
    r = lax.broadcasted_iota(jnp.int32, (tm, tm), 0)
    cidx = lax.broadcasted_iota(jnp.int32, (tm, tm), 1)
    upper = (r < cidx).astype(BF16)
    cum = jnp.dot(chosen_f.astype(BF16), upper, preferred_element_type=F32) + carry_ref[:, 0:1]
    for k in range(TOP_K):
        idx_ref[k:k + 1, :] = idxs[k]
        gate_ref[k:k + 1, :] = exps[k] / den
        rank_ref[k:k + 1, :] = jnp.sum(jnp.where(hots[k], cum, 0.0), axis=0, keepdims=True).astype(jnp.int32)
    carry_ref[...] = carry_ref[...] + jnp.sum(chosen_f, axis=1, keepdims=True)
    cnt_ref[...] = carry_ref[...]


def _outproj(y, c, res, wy, wc, g, wr_t, br, cnt_in, *, xn_rows, xn_buf=None, row_off=0):
    t, d = res.shape
    assert d == SUBLANES * LANES
    tm = min(512, t)
    assert t % tm == 0 and row_off % tm == 0
    off = row_off // tm
    row = lambda n: pl.BlockSpec((tm, n), lambda i: (i, 0))
    col = pl.BlockSpec((TOP_K, tm), lambda i: (0, i))
    full = lambda a: pl.BlockSpec(a.shape, lambda i: (0,) * a.ndim)
    in_specs = [row(y.shape[1]), row(c.shape[1]), row(d), full(wy), full(wc), full(g), full(wr_t),
                full(br), full(cnt_in)]
    args = [y, c, res, wy, wc, g, wr_t, br, cnt_in]
    aliases = {}
    if xn_buf is not None:
        in_specs.append(pl.BlockSpec(memory_space=pl.ANY))
        args.append(xn_buf)
        aliases = {len(args) - 1: 1}
    return pl.pallas_call(
        _outproj_kernel,
        grid=(t // tm,),
        in_specs=in_specs,
        out_specs=[row(d), pl.BlockSpec((tm * SUBLANES, LANES), lambda i: (i + off, 0)), col, col, col,
                   full(cnt_in)],
        out_shape=[jax.ShapeDtypeStruct((t, d), F32),
                   jax.ShapeDtypeStruct((xn_rows * SUBLANES, LANES), F32),
                   jax.ShapeDtypeStruct((TOP_K, t), jnp.int32),
                   jax.ShapeDtypeStruct((TOP_K, t), F32),
                   jax.ShapeDtypeStruct((TOP_K, t), jnp.int32),
                   jax.ShapeDtypeStruct(cnt_in.shape, F32)],
        scratch_shapes=[pltpu.VMEM(cnt_in.shape, F32)],
        input_output_aliases=aliases,
        compiler_params=_cparams(("arbitrary",)),
        name="outproj_router",
    )(*args)


_INV_GROUP = 16


def _inverse_kernel(pad_lo_ref, pad_hi_ref, dest_ref, inv_ref, *, pad_base, fill_groups):
    step = pl.program_id(0)

    @pl.when(step == 0)
    def _():
        def sentinel(d, carry):
            inv_ref[d] = (pad_base + jnp.bitwise_and(d, MOE_BLOCK - 1)) * TOP_K
            return carry

        for e in range(pad_lo_ref.shape[0]):
            lax.fori_loop(pad_lo_ref[e], pad_hi_ref[e], sentinel, 0)

    @pl.when(step > 0)
    def _():
        def fill(jj, carry):
            a0 = ((step - 1) * fill_groups + jj) * _INV_GROUP
            for c in range(_INV_GROUP):
                inv_ref[dest_ref[a0 + c]] = a0 + c
            return carry

        lax.fori_loop(0, fill_groups, fill, 0)


def _build_inverse(dest, pad_lo, pad_hi, n_rows, pad_base):
    n_fill = dest.shape[0] // _INV_GROUP
    assert dest.shape[0] % _INV_GROUP == 0
    slices = max(g for g in range(1, 17) if n_fill % g == 0)
    kern = functools.partial(_inverse_kernel, pad_base=pad_base, fill_groups=n_fill // slices)
    smem = pl.BlockSpec(memory_space=pltpu.SMEM)
    return pl.pallas_call(
        kern,
        grid=(1 + slices,),
        in_specs=[smem, smem, smem],
        out_specs=smem,
        out_shape=jax.ShapeDtypeStruct((n_rows,), jnp.int32),
        compiler_params=_cparams(("arbitrary",)),
        name="inverse_table",
    )(pad_lo, pad_hi, dest)


def _moe_kernel(be_ref, na_ref, tok_ref, row_ref, xn_hbm, wg_ref, bg_ref, wu_ref, bu_ref, wd_ref, bd_ref,
                yk_hbm, wg_s, wu_s, wd_s, x0, x1, o0, o1, gsem, ssem, *, trash_row):
    i = pl.program_id(0)
    n_active = na_ref[0]
    tab_rows = na_ref[1]
    last_block = pl.num_programs(0) - 2
    xbuf, obuf = (x0, x1), (o0, o1)
    prev = be_ref[jnp.maximum(i - 1, 0)]
    new_expert = jnp.logical_or(i == 0, be_ref[i] != prev)
    active = i < n_active

    def gather_start(block, slot):
        def issue(h, carry):
            for c in range(LANES):
                src = pl.multiple_of(tok_ref[block * tab_rows + h, c], SUBLANES)
                dst = pl.multiple_of((h * LANES + c) * SUBLANES, SUBLANES)
                pltpu.make_async_copy(xn_hbm.at[pl.ds(src, SUBLANES), :],
                                      xbuf[slot].at[pl.ds(dst, SUBLANES), :],
                                      gsem.at[slot]).start(priority=c % 2)
            return carry

        lax.fori_loop(0, tab_rows, issue, 0)

    def gather_wait(slot):
        pltpu.make_async_copy(xn_hbm.at[pl.ds(0, MOE_BLOCK * SUBLANES), :], xbuf[slot],
                              gsem.at[slot]).wait()

    def scatter_start(block, slot, real, trash):
        for r in range(MOE_BLOCK):
            dst = jnp.where(real, row_ref[block * (MOE_BLOCK // LANES) + r // LANES, r % LANES],
                            (trash + r) * SUBLANES)
            pltpu.make_async_copy(obuf[slot].at[pl.ds(r * SUBLANES, SUBLANES), :],
                                  yk_hbm.at[pl.ds(pl.multiple_of(dst, SUBLANES), SUBLANES), :],
                                  ssem.at[slot]).start(priority=r % 2)

    def scatter_wait(slot):
        pltpu.make_async_copy(obuf[slot], yk_hbm.at[pl.ds(0, MOE_BLOCK * SUBLANES), :],
                              ssem.at[slot]).wait()

    @pl.when(i == 0)
    def _():
        o0[...] = jnp.zeros_like(o0)
        o1[...] = jnp.zeros_like(o1)
        gather_start(0, 0)
        scatter_start(0, 0, False, trash_row)

    @pl.when(jnp.logical_and(active, new_expert))
    def _():
        wg_s[...] = wg_ref[0].astype(BF16)
        wu_s[...] = wu_ref[0].astype(BF16)
        wd_s[...] = wd_ref[0].astype(BF16)

    for p in (0, 1):
        mine = (i % 2) == p

        @pl.when(jnp.logical_and(active, mine))
        def _():
            gather_wait(p)
            gather_start(jnp.minimum(i + 1, last_block), 1 - p)

        @pl.when(jnp.logical_and(active, mine))
        def _():
            scatter_start(jnp.maximum(i - 1, 0), 1 - p, i > 0, trash_row + MOE_BLOCK)
            x = _tiles_to_rows(xbuf[p], MOE_BLOCK).astype(BF16)
            gt = jnp.minimum(jnp.dot(x, wg_s[...], preferred_element_type=F32) + bg_ref[0], SWIGLU_LIMIT)
            up = jnp.clip(jnp.dot(x, wu_s[...], preferred_element_type=F32) + bu_ref[0],
                          -SWIGLU_LIMIT, SWIGLU_LIMIT)
            hdn = (up + 1.0) * gt * _sigmoid(gt * SWIGLU_ALPHA)
            y = jnp.dot(hdn.astype(BF16), wd_s[...], preferred_element_type=F32) + bd_ref[0]
            scatter_wait(p)
            _rows_to_tiles(obuf[p], y)

        @pl.when(jnp.logical_and(i == n_active, mine))
        def _():
            gather_wait(p)
            scatter_wait(p)
            scatter_start(i - 1, 1 - p, True, trash_row)
            scatter_wait(1 - p)


def _moe(block_expert, n_active, tok2d, row2d, xn, w_gate, b_gate, w_up, b_up, w_down, b_down, *,
         yk_rows, trash_row):
    d = w_gate.shape[1]
    assert d == SUBLANES * LANES and xn.shape[1] == LANES
    nb = tok2d.shape[0] * LANES // MOE_BLOCK
    dff = w_gate.shape[2]
    wspec = lambda a: pl.BlockSpec((1,) + a.shape[1:], lambda i, be, na, tk, rw: (be[i], 0, 0))
    blk = pltpu.VMEM((MOE_BLOCK * SUBLANES, LANES), F32)
    kern = functools.partial(_moe_kernel, trash_row=trash_row)
    return pl.pallas_call(
        kern,
        grid_spec=pltpu.PrefetchScalarGridSpec(
            num_scalar_prefetch=4,
            grid=(nb + 1,),
            in_specs=[pl.BlockSpec(memory_space=pl.ANY), wspec(w_gate), wspec(b_gate), wspec(w_up),
                      wspec(b_up), wspec(w_down), wspec(b_down)],
            out_specs=pl.BlockSpec(memory_space=pl.ANY),
            scratch_shapes=[pltpu.VMEM((d, dff), BF16), pltpu.VMEM((d, dff), BF16),
                            pltpu.VMEM((dff, d), BF16), blk, blk, blk, blk,
                            pltpu.SemaphoreType.DMA((2,)), pltpu.SemaphoreType.DMA((2,))],
        ),
        out_shape=jax.ShapeDtypeStruct((yk_rows * SUBLANES, LANES), F32),
        compiler_params=_cparams(("arbitrary",)),
        name="moe_experts",
    )(block_expert, n_active, tok2d, row2d, xn, w_gate, b_gate, w_up, b_up, w_down, b_down)


def _combine_kernel(h_ref, gate_ref, gf_ref, *rest):
    yk_refs, o_ref = rest[:TOP_K], rest[TOP_K]
    acc = h_ref[...]
    tm = acc.shape[0]
    for k in range(TOP_K):
        acc = acc + gate_ref[:, k:k + 1] * _tiles_to_rows(yk_refs[k], tm)
    o_ref[...] = _rms(acc, gf_ref[...])


def _combine(h, gates, g_final, yk, row_off, plane_tokens):
    t, d = h.shape
    tm = min(512, t)
    assert t % tm == 0
    plane = lambda k: pl.BlockSpec(
        (pl.Element(tm * SUBLANES), pl.Element(LANES)),
        lambda i: ((k * plane_tokens + row_off + i * tm) * SUBLANES, 0))
    return pl.pallas_call(
        _combine_kernel,
        grid=(t // tm,),
        in_specs=[pl.BlockSpec((tm, d), lambda i: (i, 0)),
                  pl.BlockSpec((tm, TOP_K), lambda i: (i, 0)),
                  pl.BlockSpec(g_final.shape, lambda i: (0, 0))] + [plane(k) for k in range(TOP_K)],
        out_specs=pl.BlockSpec((tm, d), lambda i: (i, 0)),
        out_shape=jax.ShapeDtypeStruct((t, d), F32),
        compiler_params=_cparams(("parallel",)),
        name="combine",
    )(h, gates, g_final, *([yk] * TOP_K))


def _pad_lanes(v, fill=0.0):
    v = v.reshape(1, -1).astype(F32)
    return jnp.pad(v, ((0, 0), (0, LANES - v.shape[1])), constant_values=fill)


def kernel(x_prompt, x_sample, state_ssm, state_ssd_conv, state_conf_conv, meta_tokens, g_mix, w_in,
           conv_ssd_w, conv_ssd_b, dt_bias, a_log, d_skip, g_ssd_norm, conv_conf_w, conv_conf_b,
           ln_conf_g, ln_conf_b, w_out, g_ffn, w_router, b_router, w_gate, b_gate, w_up, b_up,
           w_down, b_down, g_final):
    bp, seq, d = x_prompt.shape
    bs = x_sample.shape[0]
    depth = w_in.shape[0]
    assert depth == 1 and x_sample.shape[1] == 1 and seq % CHUNK == 0
    n_heads = a_log.shape[1]
    d_ssm = n_heads * SSD_HEAD_DIM
    conv_dim = conv_ssd_w.shape[2]
    dc = conv_conf_w.shape[2]
    l = 0

    o1, o2, o3 = d_ssm, d_ssm + conv_dim, d_ssm + conv_dim + n_heads
    w_in_l = w_in[l]
    wz = w_in_l[:, :o1].astype(BF16)
    wx = w_in_l[:, o1:o2].astype(BF16)
    wdt = jnp.pad(w_in_l[:, o2:o3], ((0, 0), (0, LANES - n_heads))).astype(BF16)
    wglu = w_in_l[:, o3:].astype(BF16)
    row2 = lambda v: v.reshape(1, -1).astype(F32)
    seq_params = (
        conv_ssd_w[l], row2(conv_ssd_b[l]), _pad_lanes(dt_bias[l]), _pad_lanes(a_log[l]),
        row2(jnp.repeat(d_skip[l], SSD_HEAD_DIM)), row2(g_ssd_norm[l]),
        jnp.pad(conv_conf_w[l], ((0, 32 - CONF_WIDTH), (0, 0))), row2(conv_conf_b[l]),
        row2(ln_conf_g[l]), row2(ln_conf_b[l]))
    g_mix_r = row2(g_mix[l])

    xp2 = x_prompt.reshape(bp * seq, d)
    front_pad = (-N_META) % CHUNK
    small = jnp.concatenate([jnp.zeros((front_pad, d), F32), meta_tokens.astype(F32),
                             x_sample.reshape(bs, d)], axis=0)
    z_p, xbc_p, glu_p, dt_p = _inproj(xp2, g_mix_r, wz, wx, wglu, wdt)
    z_m, xbc_m, glu_m, dt_m = _inproj(small, g_mix_r, wz, wx, wglu, wdt)

    r3 = lambda a: a.reshape(bp, seq, a.shape[-1])
    y_p, c_p, ssm_p, shist_p, chist_p = _seq(
        r3(xbc_p), r3(z_p), r3(glu_p), r3(dt_p),
        xbc_m[:CHUNK], z_m[:CHUNK], glu_m[:CHUNK], dt_m[:CHUNK], seq_params, n_heads=n_heads)

    y_s, c_s, ssm_s, shist_s, chist_s = _sample(
        xbc_m[CHUNK:], z_m[CHUNK:], glu_m[CHUNK:], dt_m[CHUNK:],
        state_ssm[l], state_ssd_conv[l], state_conf_conv[l], seq_params, n_heads=n_heads)

    w_out_l = w_out[l]
    wy = w_out_l[:d_ssm].astype(BF16)
    wc = w_out_l[d_ssm:].astype(BF16)
    g_ffn_r = row2(g_ffn[l])
    wr_t = w_router[l].T.astype(F32)
    br = b_router[l].reshape(N_EXPERTS, 1).astype(F32)
    cnt0 = jnp.zeros((N_EXPERTS, LANES), F32)
    t_p = bp * seq
    t_all = t_p + bs
    assert (t_all * TOP_K) % LANES == 0
    xn_all = jnp.zeros((t_all * SUBLANES, LANES), F32)
    h_p, xn_all, idx_p, gate_p, rank_p, cnt1 = _outproj(
        y_p.reshape(t_p, d_ssm), c_p.reshape(t_p, dc), xp2, wy, wc, g_ffn_r, wr_t, br, cnt0,
        xn_rows=t_all, xn_buf=xn_all)
    h_s, xn_all, idx_s, gate_s, rank_s, cnt2 = _outproj(
        y_s, c_s, x_sample.reshape(bs, d), wy, wc, g_ffn_r, wr_t, br, cnt1,
        xn_rows=t_all, xn_buf=xn_all, row_off=t_p)

    counts = cnt2[:, 0].astype(jnp.int32)
    padded = (counts + MOE_BLOCK - 1) // MOE_BLOCK * MOE_BLOCK
    pad_end = jnp.cumsum(padded)
    pad_start = pad_end - padded
    n_blocks = -(-(t_all * TOP_K) // MOE_BLOCK) + N_EXPERTS
    rows = n_blocks * MOE_BLOCK
    blk_first = jnp.arange(n_blocks + 1, dtype=jnp.int32) * MOE_BLOCK
    block_expert = jnp.minimum(jnp.sum(pad_end[None, :] <= blk_first[:, None], axis=1),
                               N_EXPERTS - 1).astype(jnp.int32)
    n_active = jnp.stack([pad_end[-1] // MOE_BLOCK,
                          jnp.asarray(MOE_BLOCK // LANES, jnp.int32)]).astype(jnp.int32)
    idx_all = jnp.concatenate([idx_p, idx_s], axis=1)
    rank_all = jnp.concatenate([rank_p, rank_s], axis=1)
    experts = jnp.arange(N_EXPERTS, dtype=jnp.int32)[:, None, None]
    start_of = jnp.sum(jnp.where(idx_all[None] == experts, pad_start[:, None, None], 0), axis=0)
    dest = (start_of + rank_all).T.reshape(-1).astype(jnp.int32)

    trash = TOP_K * t_all
    pad_lo = jnp.concatenate([pad_start + counts, pad_end[-1:]]).astype(jnp.int32)
    pad_hi = jnp.concatenate([pad_end, jnp.full((1,), rows, jnp.int32)]).astype(jnp.int32)
    inv = _build_inverse(dest, pad_lo, pad_hi, rows, t_all).reshape(rows // LANES, LANES)
    inv_tok = inv // TOP_K
    tok2d = jnp.minimum(inv_tok, t_all - 1) * SUBLANES
    row2d = jnp.where(inv_tok < t_all, (inv % TOP_K) * t_all + inv_tok, trash + inv_tok - t_all) * SUBLANES
    yk = _moe(block_expert, n_active, tok2d, row2d, xn_all,
              w_gate[l], b_gate[l].reshape(N_EXPERTS, 1, -1), w_up[l], b_up[l].reshape(N_EXPERTS, 1, -1),
              w_down[l], b_down[l].reshape(N_EXPERTS, 1, -1), yk_rows=trash + 2 * MOE_BLOCK, trash_row=trash)
    g_fin = row2(g_final)
    y_prompt = _combine(h_p, gate_p.T, g_fin, yk, 0, t_all).reshape(bp, seq, d)
    y_sample = _combine(h_s, gate_s.T, g_fin, yk, t_p, t_all).reshape(bs, 1, d)

    return (y_prompt, y_sample, ssm_p[None], shist_p[None], chist_p[None],
            ssm_s[None], shist_s[None], chist_s[None])
```

```python
import functools

import jax
import jax.numpy as jnp
from jax import lax
from jax.experimental import pallas as pl
from jax.experimental.pallas import tpu as pltpu

F32 = jnp.float32
BF16 = jnp.bfloat16
HIGHEST = lax.Precision.HIGHEST

N_META = 16
SSD_HEAD_DIM = 64
SSD_GROUPS = 2
SSD_STATE = 128
SSD_CONV = 4
CHUNK = 128
CONF_WIDTH = 31
N_EXPERTS = 32
TOP_K = 4
SWIGLU_LIMIT = 7.0
SWIGLU_ALPHA = 1.702
EPS = 1e-5

LANES = 128
LANE_BITS = LANES.bit_length() - 1
SUBLANES = 8
MOE_BLOCK = 256
VMEM_LIMIT = 56 * 1024 * 1024


def _cparams(sem):
    return pltpu.CompilerParams(dimension_semantics=sem, vmem_limit_bytes=VMEM_LIMIT)


def _sigmoid(x):
    return 1.0 / (1.0 + jnp.exp(-x))


def _tiles_to_rows(ref, n):
    return jnp.concatenate([ref[pl.ds(s, n, stride=SUBLANES), :] for s in range(SUBLANES)], axis=1)


def _rows_to_tiles(ref, val):
    n = val.shape[0]
    for s in range(SUBLANES):
        ref[pl.ds(s, n, stride=SUBLANES), :] = val[:, s * LANES:(s + 1) * LANES]


def _silu(x):
    return x * _sigmoid(x)


def _softplus(x):
    return jnp.maximum(x, 0.0) + jnp.log1p(jnp.exp(-jnp.abs(x)))


def _rms(x, g):
    return x * lax.rsqrt(jnp.mean(x * x, axis=-1, keepdims=True) + EPS) * g


def _inproj_kernel(x_ref, g_ref, wz_ref, wx_ref, wg_ref, wd_ref, z_ref, xbc_ref, glu_ref, dt_ref):
    u = _rms(x_ref[...], g_ref[...]).astype(BF16)
    z_ref[...] = jnp.dot(u, wz_ref[...], preferred_element_type=F32).astype(BF16)
    xbc_ref[...] = jnp.dot(u, wx_ref[...], preferred_element_type=F32).astype(BF16)
    glu_ref[...] = jnp.dot(u, wg_ref[...], preferred_element_type=F32).astype(BF16)
    dt_ref[...] = jnp.dot(u, wd_ref[...], preferred_element_type=F32)


def _inproj(x, g, wz, wx, wg, wd):
    t, d = x.shape
    tm = min(512, t)
    assert t % tm == 0
    row = lambda n: pl.BlockSpec((tm, n), lambda i: (i, 0))
    full = lambda a: pl.BlockSpec(a.shape, lambda i: (0, 0))
    return pl.pallas_call(
        _inproj_kernel,
        grid=(t // tm,),
        in_specs=[row(d), full(g), full(wz), full(wx), full(wg), full(wd)],
        out_specs=[row(wz.shape[1]), row(wx.shape[1]), row(wg.shape[1]), row(wd.shape[1])],
        out_shape=[jax.ShapeDtypeStruct((t, wz.shape[1]), BF16),
                   jax.ShapeDtypeStruct((t, wx.shape[1]), BF16),
                   jax.ShapeDtypeStruct((t, wg.shape[1]), BF16),
                   jax.ShapeDtypeStruct((t, wd.shape[1]), F32)],
        compiler_params=_cparams(("parallel",)),
        name="inproj",
    )(x, g, wz, wx, wg, wd)


def _seq_kernel(x_ref, xbcm_ref, zm_ref, glum_ref, dtm_ref, gmix_ref, wz_ref, wx_ref, wg_ref, wd_ref,
                cw_ref, cb_ref, dtb_ref, alog_ref, dskip_ref, gn_ref,
                ccw_ref, ccb_ref, lng_ref, lnb_ref, wy_ref, wc_ref,
                h_ref, ssm_ref, shist_ref, chist_ref,
                st_ref, xbuf_ref, cbuf_ref, ctile_ref, *, d_ssm, n_heads, front_pad):
    c = pl.program_id(1)
    last = pl.num_programs(1) - 1
    is_meta = c == 0
    q = CHUNK
    n = SSD_STATE
    gw = SSD_GROUPS * n

    @pl.when(is_meta)
    def _():
        st_ref[...] = jnp.zeros_like(st_ref)
        xbuf_ref[...] = jnp.zeros_like(xbuf_ref)
        cbuf_ref[...] = jnp.zeros_like(cbuf_ref)

    x_in = x_ref[0]
    u = _rms(x_in, gmix_ref[...]).astype(BF16)
    proj = lambda w_ref, m_ref: jnp.where(is_meta, m_ref[...].astype(F32),
                                          jnp.dot(u, w_ref[...], preferred_element_type=F32))

    xbc_raw = proj(wx_ref, xbcm_ref)
    xbuf_ref[SUBLANES:SUBLANES + q, :] = xbc_raw
    acc = cb_ref[...] + cw_ref[SSD_CONV - 1:SSD_CONV, :] * xbc_raw
    for j in range(1, SSD_CONV):
        acc = acc + cw_ref[SSD_CONV - 1 - j:SSD_CONV - j, :] * xbuf_ref[SUBLANES - j:SUBLANES - j + q, :]
    xbuf_ref[0:SUBLANES, :] = xbuf_ref[q:q + SUBLANES, :]
    xact = _silu(acc)
    xs = xact[:, :d_ssm]
    bm = [xact[:, d_ssm + g * n:d_ssm + (g + 1) * n] for g in range(SSD_GROUPS)]
    cm = [xact[:, d_ssm + gw + g * n:d_ssm + gw + (g + 1) * n] for g in range(SSD_GROUPS)]

    dt_raw = proj(wd_ref, dtm_ref)
    rows = lax.broadcasted_iota(jnp.int32, (q, LANES), 0)
    cols = lax.broadcasted_iota(jnp.int32, (q, LANES), 1)
    valid = jnp.logical_or(jnp.logical_not(is_meta), rows >= front_pad)
    dtv = jnp.where(valid, _softplus(dt_raw + dtb_ref[...]), 0.0)
    da = dtv * (-jnp.exp(alog_ref[...]))
    causal = rows >= cols
    tri = causal.astype(F32)
    a_cum = jnp.dot(tri, da, preferred_element_type=F32, precision=HIGHEST)
    a_cum_t = a_cum.T
    dt_t = dtv.T
    w_t = jnp.exp(a_cum_t[:, q - 1:q] - a_cum_t) * dt_t
    a_last = a_cum[q - 1:q, :]

    bm_t = [b.T for b in bm]
    cb = [lax.dot_general(cm[g].astype(BF16), bm[g].astype(BF16), (((1,), (1,)), ((), ())),
                          preferred_element_type=F32) for g in range(SSD_GROUPS)]
    lane = lax.broadcasted_iota(jnp.int32, (q, LANES), 1)
    lo = lane < SSD_HEAD_DIM
    heads_per_group = n_heads // SSD_GROUPS

    y_parts = []
    for j in range(n_heads // 2):
        g = (2 * j) // heads_per_group
        m_l, ce_l, bw_l = [], [], []
        for h in (2 * j, 2 * j + 1):
            col = a_cum[:, h:h + 1]
            seg = col - a_cum_t[h:h + 1, :]
            dec = jnp.exp(jnp.where(causal, seg, -jnp.inf))
            m_l.append((cb[g] * dec * dt_t[h:h + 1, :]).astype(BF16))
            ce_l.append((cm[g] * jnp.exp(col)).astype(BF16))
            bw_l.append((bm_t[g] * w_t[h:h + 1, :]).astype(BF16))
        sl = slice(j * LANES, (j + 1) * LANES)
        xs_pair = xs[:, sl]
        rhs_x = jnp.concatenate([jnp.where(lo, xs_pair, 0.0), jnp.where(lo, 0.0, xs_pair)],
                                axis=0).astype(BF16)
        st_pair = st_ref[:, sl]
        rhs_s = jnp.concatenate([jnp.where(lo, st_pair, 0.0), jnp.where(lo, 0.0, st_pair)],
                                axis=0).astype(BF16)
        lhs_y = jnp.concatenate(m_l + ce_l, axis=1)
        y_pair = jnp.dot(lhs_y, jnp.concatenate([rhs_x, rhs_s], axis=0), preferred_element_type=F32)
        st_new = jnp.dot(jnp.concatenate(bw_l, axis=1), rhs_x, preferred_element_type=F32)
        cd = jnp.exp(jnp.where(lo[0:1, :], a_last[:, 2 * j:2 * j + 1], a_last[:, 2 * j + 1:2 * j + 2]))
        st_ref[:, sl] = st_pair * cd + st_new
        y_parts.append(y_pair + dskip_ref[:, sl] * xs_pair)

    y = jnp.concatenate(y_parts, axis=1)
    zf = proj(wz_ref, zm_ref)
    y = y * _silu(zf)
    gsz = d_ssm // SSD_GROUPS
    y = jnp.concatenate(
        [_rms(y[:, g * gsz:(g + 1) * gsz], gn_ref[:, g * gsz:(g + 1) * gsz]) for g in range(SSD_GROUPS)],
        axis=1)

    glu = proj(wg_ref, glum_ref)
    dc = glu.shape[1] // 2
    cval = glu[:, :dc] * _sigmoid(glu[:, dc:])
    hist = CONF_WIDTH - 1
    base = 32
    tile = lambda t: t * SUBLANES
    _rows_to_tiles(cbuf_ref.at[pl.ds(tile(base), tile(q)), :], cval)
    taps = ccw_ref[...].reshape(base, SUBLANES, LANES)
    cacc = jnp.broadcast_to(ccb_ref[...][None], (q, SUBLANES, LANES))
    for k in range(CONF_WIDTH):
        off = base - hist + k
        window = cbuf_ref[pl.ds(tile(off), tile(q)), :].reshape(q, SUBLANES, LANES)
        cacc = cacc + taps[k][None] * window
    cbuf_ref[0:tile(base), :] = cbuf_ref[tile(q):tile(q + base), :]
    ctile_ref[...] = cacc.reshape(tile(q), LANES)
    cacc = _tiles_to_rows(ctile_ref, q)
    mu = jnp.mean(cacc, axis=-1, keepdims=True)
    var = jnp.mean(jnp.square(cacc - mu), axis=-1, keepdims=True)
    cn = (cacc - mu) * lax.rsqrt(var + EPS) * lng_ref[...] + lnb_ref[...]
    cout = _silu(cn)

    h_ref[0] = (x_in + jnp.dot(y.astype(BF16), wy_ref[...], preferred_element_type=F32)
                + jnp.dot(cout.astype(BF16), wc_ref[...], preferred_element_type=F32))

    @pl.when(c == last)
    def _():
        ssm_ref[0] = st_ref[...].T.reshape(n_heads, SSD_HEAD_DIM, n)
        shist_ref[0] = xbuf_ref[SUBLANES - (SSD_CONV - 1):SUBLANES, :]
        chist_ref[0] = _tiles_to_rows(cbuf_ref.at[pl.ds(tile(base - hist), tile(hist)), :], hist)


def _seq(x, meta_proj, g_mix, w_in_parts, params, w_out_parts, *, n_heads):
    b, seq, d = x.shape
    xbcm, zm, glum, dtm = meta_proj
    wz, wx, wg, wd = w_in_parts
    conv_dim, d_ssm, dc = wx.shape[1], wz.shape[1], wg.shape[1] // 2
    assert dc == SUBLANES * LANES
    nc = seq // CHUNK + 1
    front_pad = (-N_META) % CHUNK
    blk = lambda n: pl.BlockSpec((1, CHUNK, n), lambda i, c: (i, jnp.maximum(c - 1, 0), 0))
    full = lambda a: pl.BlockSpec(a.shape, lambda i, c: (0,) * a.ndim)
    consts = [xbcm, zm, glum, dtm, g_mix, wz, wx, wg, wd, *params, *w_out_parts]
    kern = functools.partial(_seq_kernel, d_ssm=d_ssm, n_heads=n_heads, front_pad=front_pad)
    return pl.pallas_call(
        kern,
        grid=(b, nc),
        in_specs=[blk(d)] + [full(a) for a in consts],
        out_specs=[blk(d),
                   pl.BlockSpec((1, n_heads, SSD_HEAD_DIM, SSD_STATE), lambda i, c: (i, 0, 0, 0)),
                   pl.BlockSpec((1, SSD_CONV - 1, conv_dim), lambda i, c: (i, 0, 0)),
                   pl.BlockSpec((1, CONF_WIDTH - 1, dc), lambda i, c: (i, 0, 0))],
        out_shape=[jax.ShapeDtypeStruct((b, seq, d), F32),
                   jax.ShapeDtypeStruct((b, n_heads, SSD_HEAD_DIM, SSD_STATE), F32),
                   jax.ShapeDtypeStruct((b, SSD_CONV - 1, conv_dim), F32),
                   jax.ShapeDtypeStruct((b, CONF_WIDTH - 1, dc), F32)],
        scratch_shapes=[pltpu.VMEM((SSD_STATE, d_ssm), F32),
                        pltpu.VMEM((SUBLANES + CHUNK, conv_dim), F32),
                        pltpu.VMEM(((32 + CHUNK) * SUBLANES, LANES), F32),
                        pltpu.VMEM((CHUNK * SUBLANES, LANES), F32)],
        compiler_params=_cparams(("parallel", "arbitrary")),
        name="seq",
    )(x, *consts)


def _sample_kernel(xbc_ref, z_ref, glu_ref, dt_ref, ssm_ref, shist_ref, chist_ref,
                   cw_ref, cb_ref, dtb_ref, alog_ref, dskip_ref, gn_ref,
                   ccw_ref, ccb_ref, lng_ref, lnb_ref,
                   y_ref, c_ref, ssm_o, shist_o, chist_o, *, d_ssm, n_heads):
    sb = xbc_ref.shape[0]
    n = SSD_STATE
    gw = SSD_GROUPS * n
    hp = d_ssm // SSD_GROUPS
    heads_per_group = n_heads // SSD_GROUPS

    x_new = xbc_ref[...].astype(F32)
    acc = cb_ref[...] + cw_ref[SSD_CONV - 1:SSD_CONV, :] * x_new
    for k in range(SSD_CONV - 1):
        acc = acc + cw_ref[k:k + 1, :] * shist_ref[:, k, :]
    for k in range(SSD_CONV - 2):
        shist_o[:, k, :] = shist_ref[:, k + 1, :]
    shist_o[:, SSD_CONV - 2, :] = x_new
    xact = _silu(acc)
    xs = xact[:, :d_ssm]
    dtv = _softplus(dt_ref[...] + dtb_ref[...])
    a = -jnp.exp(alog_ref[...])
    decay = jnp.exp(dtv * a)

    hsel = (lax.broadcasted_iota(jnp.int32, (LANES, d_ssm), 1) // SSD_HEAD_DIM
            == lax.broadcasted_iota(jnp.int32, (LANES, d_ssm), 0)).astype(F32)
    dt_x = jnp.dot(dtv, hsel, preferred_element_type=F32, precision=HIGHEST)
    xdt = (xs * dt_x).astype(BF16)
    rowid = lax.broadcasted_iota(jnp.int32, (sb, 1), 0)

    y_rows = []
    for i in range(sb):
        sel = rowid == i
        xi = jnp.where(sel, xdt, jnp.zeros_like(xdt))
        parts = []
        for g in range(SSD_GROUPS):
            bm = xact[:, d_ssm + g * n:d_ssm + (g + 1) * n].astype(BF16)
            cmat = xact[:, d_ssm + gw + g * n:d_ssm + gw + (g + 1) * n].astype(BF16)
            outer = lax.dot_general(xi[:, g * hp:(g + 1) * hp], bm, (((0,), (0,)), ((), ())),
                                    preferred_element_type=F32)
            outer = outer.reshape(heads_per_group, SSD_HEAD_DIM, n)
            hs = slice(g * heads_per_group, (g + 1) * heads_per_group)
            s_old = ssm_ref[i, hs]
            dec_i = jnp.stack([jnp.broadcast_to(decay[i:i + 1, h:h + 1], (SSD_HEAD_DIM, n))
                               for h in range(g * heads_per_group, (g + 1) * heads_per_group)])
            s_new = s_old * dec_i + outer
            ssm_o[i, hs] = s_new
            yv = lax.dot_general(cmat, s_new.reshape(hp, n).astype(BF16), (((1,), (1,)), ((), ())),
                                 preferred_element_type=F32)
            parts.append(yv)
        yfull = jnp.concatenate(parts, axis=1)
        y_rows.append(jnp.where(sel, yfull, 0.0))
    y = y_rows[0]
    for r in y_rows[1:]:
        y = y + r
    y = y + dskip_ref[...] * xs
    y = y * _silu(z_ref[...].astype(F32))
    gsz = d_ssm // SSD_GROUPS
    y = jnp.concatenate(
        [_rms(y[:, g * gsz:(g + 1) * gsz], gn_ref[:, g * gsz:(g + 1) * gsz]) for g in range(SSD_GROUPS)],
        axis=1)
    y_ref[...] = y.astype(BF16)

    glu = glu_ref[...].astype(F32)
    dc = glu.shape[1] // 2
    cval = glu[:, :dc] * _sigmoid(glu[:, dc:])
    hist = CONF_WIDTH - 1
    cacc = ccb_ref[...] + ccw_ref[hist:hist + 1, :] * cval
    cacc = cacc + jnp.sum(chist_ref[...] * ccw_ref[0:hist, :][None], axis=1)
    chist_o[:, 0:hist - 1, :] = chist_ref[:, 1:hist, :]
    chist_o[:, hist - 1, :] = cval
    mu = jnp.mean(cacc, axis=-1, keepdims=True)
    var = jnp.mean(jnp.square(cacc - mu), axis=-1, keepdims=True)
    cn = (cacc - mu) * lax.rsqrt(var + EPS) * lng_ref[...] + lnb_ref[...]
    c_ref[...] = _silu(cn).astype(BF16)


def _sample(xbc, z, glu, dt, ssm, shist, chist, params, *, n_heads):
    bs, conv_dim = xbc.shape
    d_ssm = z.shape[1]
    dc = glu.shape[1] // 2
    sb = SUBLANES
    assert bs % sb == 0
    row = lambda n: pl.BlockSpec((sb, n), lambda i: (i, 0))
    full = lambda a: pl.BlockSpec(a.shape, lambda i: (0,) * a.ndim)
    st_spec = pl.BlockSpec((sb, n_heads, SSD_HEAD_DIM, SSD_STATE), lambda i: (i, 0, 0, 0))
    sh_spec = pl.BlockSpec((sb, SSD_CONV - 1, conv_dim), lambda i: (i, 0, 0))
    ch_spec = pl.BlockSpec((sb, CONF_WIDTH - 1, dc), lambda i: (i, 0, 0))
    kern = functools.partial(_sample_kernel, d_ssm=d_ssm, n_heads=n_heads)
    return pl.pallas_call(
        kern,
        grid=(bs // sb,),
        in_specs=[row(conv_dim), row(d_ssm), row(2 * dc), row(LANES), st_spec, sh_spec, ch_spec]
        + [full(p) for p in params],
        out_specs=[row(d_ssm), row(dc), st_spec, sh_spec, ch_spec],
        out_shape=[jax.ShapeDtypeStruct((bs, d_ssm), BF16),
                   jax.ShapeDtypeStruct((bs, dc), BF16),
                   jax.ShapeDtypeStruct(ssm.shape, F32),
                   jax.ShapeDtypeStruct(shist.shape, F32),
                   jax.ShapeDtypeStruct(chist.shape, F32)],
        compiler_params=_cparams(("parallel",)),
        name="sample_step",
    )(xbc, z, glu, dt, ssm, shist, chist, *params)


def _router_kernel(*refs, with_proj):
    if with_proj:
        y_ref, c_ref, res_ref, wy_ref, wc_ref = refs[:5]
        g_ref, wr_ref, br_ref, cnt_in_ref, _, h_ref = refs[5:11]
        xn_ref, idx_ref, gate_ref, rank_ref, cnt_ref, carry_ref = refs[11:]
        h = (res_ref[...]
             + jnp.dot(y_ref[...], wy_ref[...], preferred_element_type=F32)
             + jnp.dot(c_ref[...], wc_ref[...], preferred_element_type=F32))
        h_ref[...] = h
    else:
        h_in_ref, g_ref, wr_ref, br_ref, cnt_in_ref, _ = refs[:6]
        xn_ref, idx_ref, gate_ref, rank_ref, cnt_ref, carry_ref = refs[6:]
        h = h_in_ref[...]
    i = pl.program_id(0)

    @pl.when(i == 0)
    def _():
        carry_ref[...] = cnt_in_ref[...]

    xn = _rms(h, g_ref[...])
    _rows_to_tiles(xn_ref, xn)
    tm = h.shape[0]
    logits = lax.dot_general(wr_ref[...], xn, (((1,), (1,)), ((), ())),
                             preferred_element_type=F32, precision=HIGHEST) + br_ref[...]
    eid = lax.broadcasted_iota(jnp.int32, logits.shape, 0)
    vals, idxs, hots = [], [], []
    work = logits
    for _ in range(TOP_K):
        m = jnp.max(work, axis=0, keepdims=True)
        sel = jnp.min(jnp.where(work == m, eid, N_EXPERTS), axis=0, keepdims=True)
        hot = eid == sel
        work = jnp.where(hot, -jnp.inf, work)
        vals.append(m)
        idxs.append(sel)
        hots.append(hot)
    exps = [jnp.exp(v - vals[0]) for v in vals]
    den = exps[0]
    for e in exps[1:]:
        den = den + e
    chosen = hots[0]
    for hot in hots[1:]:
        chosen = jnp.logical_or(chosen, hot)
    chosen_f = chosen.astype(F32)
    r = lax.broadcasted_iota(jnp.int32, (tm, tm), 0)
    cidx = lax.broadcasted_iota(jnp.int32, (tm, tm), 1)
    upper = (r < cidx).astype(BF16)
    cum = jnp.dot(chosen_f.astype(BF16), upper, preferred_element_type=F32) + carry_ref[:, 0:1]
    for k in range(TOP_K):
        idx_ref[k:k + 1, :] = idxs[k]
        gate_ref[k:k + 1, :] = exps[k] / den
        rank_ref[k:k + 1, :] = jnp.sum(jnp.where(hots[k], cum, 0.0), axis=0, keepdims=True).astype(jnp.int32)
    carry_ref[...] = carry_ref[...] + jnp.sum(chosen_f, axis=1, keepdims=True)
    cnt_ref[...] = carry_ref[...]


def _router(h_or_proj, g, wr_t, br, cnt_in, xn_buf, row_off):
    with_proj = isinstance(h_or_proj, tuple)
    t, d = (h_or_proj[2] if with_proj else h_or_proj).shape
    assert d == SUBLANES * LANES
    tm = min(512, t)
    assert t % tm == 0 and row_off % tm == 0
    off = row_off // tm
    row = lambda n: pl.BlockSpec((tm, n), lambda i: (i, 0))
    col = pl.BlockSpec((TOP_K, tm), lambda i: (0, i))
    full = lambda a: pl.BlockSpec(a.shape, lambda i: (0,) * a.ndim)
    if with_proj:
        y, c, res, wy, wc = h_or_proj
        args = [y, c, res, wy, wc]
        in_specs = [row(y.shape[1]), row(c.shape[1]), row(d), full(wy), full(wc)]
        out_specs, out_shape = [row(d)], [jax.ShapeDtypeStruct((t, d), F32)]
    else:
        args, in_specs, out_specs, out_shape = [h_or_proj], [row(d)], [], []
    args += [g, wr_t, br, cnt_in, xn_buf]
    in_specs += [full(g), full(wr_t), full(br), full(cnt_in), pl.BlockSpec(memory_space=pl.ANY)]
    out_specs += [pl.BlockSpec((tm * SUBLANES, LANES), lambda i: (i + off, 0)), col, col, col, full(cnt_in)]
    out_shape += [jax.ShapeDtypeStruct(xn_buf.shape, F32),
                  jax.ShapeDtypeStruct((TOP_K, t), jnp.int32),
                  jax.ShapeDtypeStruct((TOP_K, t), F32),
                  jax.ShapeDtypeStruct((TOP_K, t), jnp.int32),
                  jax.ShapeDtypeStruct(cnt_in.shape, F32)]
    return pl.pallas_call(
        functools.partial(_router_kernel, with_proj=with_proj),
        grid=(t // tm,),
        in_specs=in_specs,
        out_specs=out_specs,
        out_shape=out_shape,
        scratch_shapes=[pltpu.VMEM(cnt_in.shape, F32)],
        input_output_aliases={len(args) - 1: len(out_shape) - 5},
        compiler_params=_cparams(("arbitrary",)),
        name="router",
    )(*args)


_INV_GROUP = 16


def _inverse_kernel(pad_lo_ref, pad_hi_ref, dest_ref, inv_ref, *, pad_base, fill_groups):
    step = pl.program_id(0)

    @pl.when(step == 0)
    def _():
        def sentinel(d, carry):
            inv_ref[d] = (pad_base + jnp.bitwise_and(d, MOE_BLOCK - 1)) * TOP_K
            return carry

        for e in range(pad_lo_ref.shape[0]):
            lax.fori_loop(pad_lo_ref[e], pad_hi_ref[e], sentinel, 0)

    @pl.when(step > 0)
    def _():
        def fill(jj, carry):
            a0 = ((step - 1) * fill_groups + jj) * _INV_GROUP
            for c in range(_INV_GROUP):
                inv_ref[dest_ref[a0 + c]] = a0 + c
            return carry

        lax.fori_loop(0, fill_groups, fill, 0)


def _build_inverse(dest, pad_lo, pad_hi, n_rows, pad_base):
    n_fill = dest.shape[0] // _INV_GROUP
    assert dest.shape[0] % _INV_GROUP == 0
    slices = max(g for g in range(1, 17) if n_fill % g == 0)
    kern = functools.partial(_inverse_kernel, pad_base=pad_base, fill_groups=n_fill // slices)
    smem = pl.BlockSpec(memory_space=pltpu.SMEM)
    return pl.pallas_call(
        kern,
        grid=(1 + slices,),
        in_specs=[smem, smem, smem],
        out_specs=smem,
        out_shape=jax.ShapeDtypeStruct((n_rows,), jnp.int32),
        compiler_params=_cparams(("arbitrary",)),
        name="inverse_table",
    )(pad_lo, pad_hi, dest)


def _moe_kernel(be_ref, na_ref, tok_ref, row_ref, xn_hbm, wg_ref, bg_ref, wu_ref, bu_ref, wd_ref, bd_ref,
                yk_hbm, wg_s, wu_s, wd_s, x0, x1, o0, o1, gsem, ssem, *, trash_row):
    i = pl.program_id(0)
    n_active = na_ref[0]
    tab_rows = na_ref[1]
    last_block = pl.num_programs(0) - 2
    xbuf, obuf = (x0, x1), (o0, o1)
    prev = be_ref[jnp.maximum(i - 1, 0)]
    new_expert = jnp.logical_or(i == 0, be_ref[i] != prev)
    active = i < n_active

    def gather_start(block, slot):
        def issue(h, carry):
            for c in range(LANES):
                src = pl.multiple_of(tok_ref[block * tab_rows + h, c], SUBLANES)
                dst = pl.multiple_of((h * LANES + c) * SUBLANES, SUBLANES)
                pltpu.make_async_copy(xn_hbm.at[pl.ds(src, SUBLANES), :],
                                      xbuf[slot].at[pl.ds(dst, SUBLANES), :],
                                      gsem.at[slot]).start(priority=c % 2)
            return carry

        lax.fori_loop(0, tab_rows, issue, 0)

    def gather_wait(slot):
        pltpu.make_async_copy(xn_hbm.at[pl.ds(0, MOE_BLOCK * SUBLANES), :], xbuf[slot],
                              gsem.at[slot]).wait()

    def scatter_start(block, slot, real, trash):
        for r in range(MOE_BLOCK):
            dst = jnp.where(real, row_ref[block * (MOE_BLOCK // LANES) + r // LANES, r % LANES],
                            (trash + r) * SUBLANES)
            pltpu.make_async_copy(obuf[slot].at[pl.ds(r * SUBLANES, SUBLANES), :],
                                  yk_hbm.at[pl.ds(pl.multiple_of(dst, SUBLANES), SUBLANES), :],
                                  ssem.at[slot]).start(priority=r % 2)

    def scatter_wait(slot):
        pltpu.make_async_copy(obuf[slot], yk_hbm.at[pl.ds(0, MOE_BLOCK * SUBLANES), :],
                              ssem.at[slot]).wait()

    @pl.when(i == 0)
    def _():
        o0[...] = jnp.zeros_like(o0)
        o1[...] = jnp.zeros_like(o1)
        gather_start(0, 0)
        scatter_start(0, 0, False, trash_row)

    @pl.when(jnp.logical_and(active, new_expert))
    def _():
        wg_s[...] = wg_ref[0].astype(BF16)
        wu_s[...] = wu_ref[0].astype(BF16)
        wd_s[...] = wd_ref[0].astype(BF16)

    for p in (0, 1):
        mine = (i % 2) == p

        @pl.when(jnp.logical_and(active, mine))
        def _():
            gather_wait(p)
            gather_start(jnp.minimum(i + 1, last_block), 1 - p)

        @pl.when(jnp.logical_and(active, mine))
        def _():
            scatter_start(jnp.maximum(i - 1, 0), 1 - p, i > 0, trash_row + MOE_BLOCK)
            x = _tiles_to_rows(xbuf[p], MOE_BLOCK).astype(BF16)
            gt = jnp.minimum(jnp.dot(x, wg_s[...], preferred_element_type=F32) + bg_ref[0], SWIGLU_LIMIT)
            up = jnp.clip(jnp.dot(x, wu_s[...], preferred_element_type=F32) + bu_ref[0],
                          -SWIGLU_LIMIT, SWIGLU_LIMIT)
            hdn = (up + 1.0) * gt * _sigmoid(gt * SWIGLU_ALPHA)
            y = jnp.dot(hdn.astype(BF16), wd_s[...], preferred_element_type=F32) + bd_ref[0]
            scatter_wait(p)
            _rows_to_tiles(obuf[p], y)

        @pl.when(jnp.logical_and(i == n_active, mine))
        def _():
            gather_wait(p)
            scatter_wait(p)
            scatter_start(i - 1, 1 - p, True, trash_row)
            scatter_wait(1 - p)


def _moe(block_expert, n_active, tok2d, row2d, xn, w_gate, b_gate, w_up, b_up, w_down, b_down, *,
         yk_rows, trash_row):
    d = w_gate.shape[1]
    assert d == SUBLANES * LANES and xn.shape[1] == LANES
    nb = tok2d.shape[0] * LANES // MOE_BLOCK
    dff = w_gate.shape[2]
    wspec = lambda a: pl.BlockSpec((1,) + a.shape[1:], lambda i, be, na, tk, rw: (be[i], 0, 0))
    blk = pltpu.VMEM((MOE_BLOCK * SUBLANES, LANES), F32)
    kern = functools.partial(_moe_kernel, trash_row=trash_row)
    return pl.pallas_call(
        kern,
        grid_spec=pltpu.PrefetchScalarGridSpec(
            num_scalar_prefetch=4,
            grid=(nb + 1,),
            in_specs=[pl.BlockSpec(memory_space=pl.ANY), wspec(w_gate), wspec(b_gate), wspec(w_up),
                      wspec(b_up), wspec(w_down), wspec(b_down)],
            out_specs=pl.BlockSpec(memory_space=pl.ANY),
            scratch_shapes=[pltpu.VMEM((d, dff), BF16), pltpu.VMEM((d, dff), BF16),
                            pltpu.VMEM((dff, d), BF16), blk, blk, blk, blk,
                            pltpu.SemaphoreType.DMA((2,)), pltpu.SemaphoreType.DMA((2,))],
        ),
        out_shape=jax.ShapeDtypeStruct((yk_rows * SUBLANES, LANES), F32),
        compiler_params=_cparams(("arbitrary",)),
        name="moe_experts",
    )(block_expert, n_active, tok2d, row2d, xn, w_gate, b_gate, w_up, b_up, w_down, b_down)


def _combine_kernel(h_ref, gate_ref, gf_ref, *rest):
    yk_refs, o_ref = rest[:TOP_K], rest[TOP_K]
    acc = h_ref[...]
    tm = acc.shape[0]
    for k in range(TOP_K):
        acc = acc + gate_ref[:, k:k + 1] * _tiles_to_rows(yk_refs[k], tm)
    o_ref[...] = _rms(acc, gf_ref[...])


def _combine(h, gates, g_final, yk, row_off, plane_tokens):
    t, d = h.shape
    tm = min(512, t)
    assert t % tm == 0
    plane = lambda k: pl.BlockSpec(
        (pl.Element(tm * SUBLANES), pl.Element(LANES)),
        lambda i: ((k * plane_tokens + row_off + i * tm) * SUBLANES, 0))
    return pl.pallas_call(
        _combine_kernel,
        grid=(t // tm,),
        in_specs=[pl.BlockSpec((tm, d), lambda i: (i, 0)),
                  pl.BlockSpec((tm, TOP_K), lambda i: (i, 0)),
                  pl.BlockSpec(g_final.shape, lambda i: (0, 0))] + [plane(k) for k in range(TOP_K)],
        out_specs=pl.BlockSpec((tm, d), lambda i: (i, 0)),
        out_shape=jax.ShapeDtypeStruct((t, d), F32),
        compiler_params=_cparams(("parallel",)),
        name="combine",
    )(h, gates, g_final, *([yk] * TOP_K))


def _pad_lanes(v, fill=0.0):
    v = v.reshape(1, -1).astype(F32)
    return jnp.pad(v, ((0, 0), (0, LANES - v.shape[1])), constant_values=fill)


def kernel(x_prompt, x_sample, state_ssm, state_ssd_conv, state_conf_conv, meta_tokens, g_mix, w_in,
           conv_ssd_w, conv_ssd_b, dt_bias, a_log, d_skip, g_ssd_norm, conv_conf_w, conv_conf_b,
           ln_conf_g, ln_conf_b, w_out, g_ffn, w_router, b_router, w_gate, b_gate, w_up, b_up,
           w_down, b_down, g_final):
    bp, seq, d = x_prompt.shape
    bs = x_sample.shape[0]
    depth = w_in.shape[0]
    assert depth == 1 and x_sample.shape[1] == 1 and seq % CHUNK == 0
    n_heads = a_log.shape[1]
    d_ssm = n_heads * SSD_HEAD_DIM
    conv_dim = conv_ssd_w.shape[2]
    dc = conv_conf_w.shape[2]
    l = 0

    o1, o2, o3 = d_ssm, d_ssm + conv_dim, d_ssm + conv_dim + n_heads
    w_in_l = w_in[l]
    wz = w_in_l[:, :o1].astype(BF16)
    wx = w_in_l[:, o1:o2].astype(BF16)
    wdt = jnp.pad(w_in_l[:, o2:o3], ((0, 0), (0, LANES - n_heads))).astype(BF16)
    wglu = w_in_l[:, o3:].astype(BF16)
    row2 = lambda v: v.reshape(1, -1).astype(F32)
    seq_params = (
        conv_ssd_w[l], row2(conv_ssd_b[l]), _pad_lanes(dt_bias[l]), _pad_lanes(a_log[l]),
        row2(jnp.repeat(d_skip[l], SSD_HEAD_DIM)), row2(g_ssd_norm[l]),
        jnp.pad(conv_conf_w[l], ((0, 32 - CONF_WIDTH), (0, 0))), row2(conv_conf_b[l]),
        row2(ln_conf_g[l]), row2(ln_conf_b[l]))
    g_mix_r = row2(g_mix[l])
    w_out_l = w_out[l]
    wy = w_out_l[:d_ssm].astype(BF16)
    wc = w_out_l[d_ssm:].astype(BF16)
    prompt_params = seq_params[:6] + (
        seq_params[6].reshape(32 * SUBLANES, LANES), conv_conf_b[l].reshape(SUBLANES, LANES).astype(F32)
    ) + seq_params[8:]

    front_pad = (-N_META) % CHUNK
    small = jnp.concatenate([jnp.zeros((front_pad, d), F32), meta_tokens.astype(F32),
                             x_sample.reshape(bs, d)], axis=0)
    z_m, xbc_m, glu_m, dt_m = _inproj(small, g_mix_r, wz, wx, wglu, wdt)

    h_p, ssm_p, shist_p, chist_p = _seq(
        x_prompt, (xbc_m[:CHUNK], z_m[:CHUNK], glu_m[:CHUNK], dt_m[:CHUNK]), g_mix_r,
        (wz, wx, wglu, wdt), prompt_params, (wy, wc), n_heads=n_heads)

    y_s, c_s, ssm_s, shist_s, chist_s = _sample(
        xbc_m[CHUNK:], z_m[CHUNK:], glu_m[CHUNK:], dt_m[CHUNK:],
        state_ssm[l], state_ssd_conv[l], state_conf_conv[l], seq_params, n_heads=n_heads)

    g_ffn_r = row2(g_ffn[l])
    wr_t = w_router[l].T.astype(F32)
    br = b_router[l].reshape(N_EXPERTS, 1).astype(F32)
    cnt0 = jnp.zeros((N_EXPERTS, LANES), F32)
    t_p = bp * seq
    t_all = t_p + bs
    assert (t_all * TOP_K) % LANES == 0
    h_p = h_p.reshape(t_p, d)
    xn_all = jnp.zeros((t_all * SUBLANES, LANES), F32)
    xn_all, idx_p, gate_p, rank_p, cnt1 = _router(h_p, g_ffn_r, wr_t, br, cnt0, xn_all, 0)
    h_s, xn_all, idx_s, gate_s, rank_s, cnt2 = _router(
        (y_s, c_s, x_sample.reshape(bs, d), wy, wc), g_ffn_r, wr_t, br, cnt1, xn_all, t_p)

    counts = cnt2[:, 0].astype(jnp.int32)
    padded = (counts + MOE_BLOCK - 1) // MOE_BLOCK * MOE_BLOCK
    pad_end = jnp.cumsum(padded)
    pad_start = pad_end - padded
    n_blocks = -(-(t_all * TOP_K) // MOE_BLOCK) + N_EXPERTS
    rows = n_blocks * MOE_BLOCK
    blk_first = jnp.arange(n_blocks + 1, dtype=jnp.int32) * MOE_BLOCK
    block_expert = jnp.minimum(jnp.sum(pad_end[None, :] <= blk_first[:, None], axis=1),
                               N_EXPERTS - 1).astype(jnp.int32)
    n_active = jnp.stack([pad_end[-1] // MOE_BLOCK,
                          jnp.asarray(MOE_BLOCK // LANES, jnp.int32)]).astype(jnp.int32)
    idx_all = jnp.concatenate([idx_p, idx_s], axis=1)
    rank_all = jnp.concatenate([rank_p, rank_s], axis=1)
    experts = jnp.arange(N_EXPERTS, dtype=jnp.int32)[:, None, None]
    start_of = jnp.sum(jnp.where(idx_all[None] == experts, pad_start[:, None, None], 0), axis=0)
    dest = (start_of + rank_all).T.reshape(-1).astype(jnp.int32)

    trash = TOP_K * t_all
    pad_lo = jnp.concatenate([pad_start + counts, pad_end[-1:]]).astype(jnp.int32)
    pad_hi = jnp.concatenate([pad_end, jnp.full((1,), rows, jnp.int32)]).astype(jnp.int32)
    inv = _build_inverse(dest, pad_lo, pad_hi, rows, t_all).reshape(rows // LANES, LANES)
    inv_tok = inv // TOP_K
    tok2d = jnp.minimum(inv_tok, t_all - 1) * SUBLANES
    row2d = jnp.where(inv_tok < t_all, (inv % TOP_K) * t_all + inv_tok, trash + inv_tok - t_all) * SUBLANES
    yk = _moe(block_expert, n_active, tok2d, row2d, xn_all,
              w_gate[l], b_gate[l].reshape(N_EXPERTS, 1, -1), w_up[l], b_up[l].reshape(N_EXPERTS, 1, -1),
              w_down[l], b_down[l].reshape(N_EXPERTS, 1, -1), yk_rows=trash + 2 * MOE_BLOCK, trash_row=trash)
    g_fin = row2(g_final)
    y_prompt = _combine(h_p, gate_p.T, g_fin, yk, 0, t_all).reshape(bp, seq, d)
    y_sample = _combine(h_s, gate_s.T, g_fin, yk, t_p, t_all).reshape(bs, 1, d)

    return (y_prompt, y_sample, ssm_p[None], shist_p[None], chist_p[None],
            ssm_s[None], shist_s[None], chist_s[None])
```

```python
import functools

import jax
import jax.numpy as jnp
from jax import lax
from jax.experimental import pallas as pl
from jax.experimental.pallas import tpu as pltpu

F32 = jnp.float32
BF16 = jnp.bfloat16
HIGHEST = lax.Precision.HIGHEST

N_META = 16
SSD_HEAD_DIM = 64
SSD_GROUPS = 2
SSD_STATE = 128
SSD_CONV = 4
CHUNK = 128
CONF_WIDTH = 31
N_EXPERTS = 32
TOP_K = 4
SWIGLU_LIMIT = 7.0
SWIGLU_ALPHA = 1.702
EPS = 1e-5

LANES = 128
LANE_BITS = LANES.bit_length() - 1
MXU_COLS = 256
SUBLANES = 8
MOE_BLOCK = 256
VMEM_LIMIT = 56 * 1024 * 1024


def _cparams(sem):
    return pltpu.CompilerParams(dimension_semantics=sem, vmem_limit_bytes=VMEM_LIMIT)


def _sigmoid(x):
    return 1.0 / (1.0 + jnp.exp(-x))


def _tiles_to_rows(ref, n):
    return jnp.concatenate([ref[pl.ds(s, n, stride=SUBLANES), :] for s in range(SUBLANES)], axis=1)


def _rows_to_tiles(ref, val):
    n = val.shape[0]
    for s in range(SUBLANES):
        ref[pl.ds(s, n, stride=SUBLANES), :] = val[:, s * LANES:(s + 1) * LANES]


def _silu(x):
    return x * _sigmoid(x)


def _softplus(x):
    return jnp.maximum(x, 0.0) + jnp.log1p(jnp.exp(-jnp.abs(x)))


def _rms(x, g):
    return x * lax.rsqrt(jnp.mean(x * x, axis=-1, keepdims=True) + EPS) * g


def _inproj_kernel(x_ref, g_ref, wz_ref, wx_ref, wg_ref, wd_ref, z_ref, xbc_ref, glu_ref, dt_ref):
    u = _rms(x_ref[...], g_ref[...]).astype(BF16)
    z_ref[...] = jnp.dot(u, wz_ref[...], preferred_element_type=F32).astype(BF16)
    xbc_ref[...] = jnp.dot(u, wx_ref[...], preferred_element_type=F32).astype(BF16)
    glu_ref[...] = jnp.dot(u, wg_ref[...], preferred_element_type=F32).astype(BF16)
    dt_ref[...] = jnp.dot(u, wd_ref[...], preferred_element_type=F32)


def _inproj(x, g, wz, wx, wg, wd):
    t, d = x.shape
    tm = min(512, t)
    assert t % tm == 0
    row = lambda n: pl.BlockSpec((tm, n), lambda i: (i, 0))
    full = lambda a: pl.BlockSpec(a.shape, lambda i: (0, 0))
    return pl.pallas_call(
        _inproj_kernel,
        grid=(t // tm,),
        in_specs=[row(d), full(g), full(wz), full(wx), full(wg), full(wd)],
        out_specs=[row(wz.shape[1]), row(wx.shape[1]), row(wg.shape[1]), row(wd.shape[1])],
        out_shape=[jax.ShapeDtypeStruct((t, wz.shape[1]), BF16),
                   jax.ShapeDtypeStruct((t, wx.shape[1]), BF16),
                   jax.ShapeDtypeStruct((t, wg.shape[1]), BF16),
                   jax.ShapeDtypeStruct((t, wd.shape[1]), F32)],
        compiler_params=_cparams(("parallel",)),
        name="inproj",
    )(x, g, wz, wx, wg, wd)


def _seq_kernel(x_ref, xnext_ref, zm_ref, xbcm_ref, glum_ref, dtm_ref, gmix_ref,
                wz_ref, wx_ref, wg_ref, wd_ref, *rest, **static):
    c = pl.program_id(1)
    bufs = (rest[-8:-4], rest[-4:])
    meta = (zm_ref, xbcm_ref, glum_ref, dtm_ref)
    st_ref, xbuf_ref, cbuf_ref = rest[-12:-9]

    @pl.when(c == 0)
    def _():
        st_ref[...] = jnp.zeros_like(st_ref)
        xbuf_ref[...] = jnp.zeros_like(xbuf_ref)
        cbuf_ref[...] = jnp.zeros_like(cbuf_ref)

    for parity in (0, 1):

        @pl.when(c % 2 == parity)
        def _():
            cur, nxt = bufs[parity], bufs[1 - parity]
            proj = lambda s: jnp.where(c == 0, meta[s][...].astype(F32), cur[s][...])
            u_next = _rms(xnext_ref[0], gmix_ref[...]).astype(BF16)
            tiles = [(dst, w_ref, n0) for dst, w_ref in zip(nxt, (wz_ref, wx_ref, wg_ref, wd_ref))
                     for n0 in range(0, w_ref.shape[1], MXU_COLS)]
            pending = iter(tiles)

            def next_tile():
                item = next(pending, None)
                if item is not None:
                    dst, w_ref, n0 = item
                    n1 = min(n0 + MXU_COLS, w_ref.shape[1])
                    dst[:, n0:n1] = jnp.dot(u_next, w_ref[:, n0:n1], preferred_element_type=F32)

            _seq_chunk(proj, next_tile, x_ref, *rest[:-8], **static)
            for _ in tiles:
                next_tile()

    @pl.when(c == pl.num_programs(1) - 1)
    def _():
        ssm_ref, shist_ref, chist_ref = rest[-15:-12]
        hist = CONF_WIDTH - 1
        ssm_ref[0] = st_ref[...].T.reshape(static["n_heads"], SSD_HEAD_DIM, SSD_STATE)
        shist_ref[0] = xbuf_ref[SUBLANES - (SSD_CONV - 1):SUBLANES, :]
        chist_ref[0] = _tiles_to_rows(cbuf_ref.at[pl.ds((32 - hist) * SUBLANES, hist * SUBLANES), :], hist)


def _seq_chunk(proj, mxu_filler, x_ref,
               cw_ref, cb_ref, dtb_ref, alog_ref, dskip_ref, gn_ref,
               ccw_ref, ccb_ref, lng_ref, lnb_ref, wy_ref, wc_ref,
               h_ref, ssm_ref, shist_ref, chist_ref,
               st_ref, xbuf_ref, cbuf_ref, ctile_ref, *, d_ssm, n_heads, front_pad):
    c = pl.program_id(1)
    last = pl.num_programs(1) - 1
    is_meta = c == 0
    q = CHUNK
    n = SSD_STATE
    gw = SSD_GROUPS * n
    x_in = x_ref[0]

    xbc_raw = proj(1)
    xbuf_ref[SUBLANES:SUBLANES + q, :] = xbc_raw
    acc = cb_ref[...] + cw_ref[SSD_CONV - 1:SSD_CONV, :] * xbc_raw
    for j in range(1, SSD_CONV):
        mxu_filler()
        acc = acc + cw_ref[SSD_CONV - 1 - j:SSD_CONV - j, :] * xbuf_ref[SUBLANES - j:SUBLANES - j + q, :]
    xbuf_ref[0:SUBLANES, :] = xbuf_ref[q:q + SUBLANES, :]
    xact = _silu(acc)
    xs = xact[:, :d_ssm]
    bm = [xact[:, d_ssm + g * n:d_ssm + (g + 1) * n] for g in range(SSD_GROUPS)]
    cm = [xact[:, d_ssm + gw + g * n:d_ssm + gw + (g + 1) * n] for g in range(SSD_GROUPS)]

    dt_raw = proj(3)
    rows = lax.broadcasted_iota(jnp.int32, (q, LANES), 0)
    cols = lax.broadcasted_iota(jnp.int32, (q, LANES), 1)
    valid = jnp.logical_or(jnp.logical_not(is_meta), rows >= front_pad)
    dtv = jnp.where(valid, _softplus(dt_raw + dtb_ref[...]), 0.0)
    da = dtv * (-jnp.exp(alog_ref[...]))
    causal = rows >= cols
    tri = causal.astype(F32)
    a_cum = jnp.dot(tri, da, preferred_element_type=F32, precision=HIGHEST)
    a_cum_t = a_cum.T
    dt_t = dtv.T
    w_t = jnp.exp(a_cum_t[:, q - 1:q] - a_cum_t) * dt_t
    a_last = a_cum[q - 1:q, :]

    bm_t = [b.T for b in bm]
    cb = [lax.dot_general(cm[g].astype(BF16), bm[g].astype(BF16), (((1,), (1,)), ((), ())),
                          preferred_element_type=F32) for g in range(SSD_GROUPS)]
    lane = lax.broadcasted_iota(jnp.int32, (q, LANES), 1)
    lo = lane < SSD_HEAD_DIM
    heads_per_group = n_heads // SSD_GROUPS

    y_parts = []
    for j in range(n_heads // 2):
        g = (2 * j) // heads_per_group
        m_l, ce_l, bw_l = [], [], []
        for h in (2 * j, 2 * j + 1):
            col = a_cum[:, h:h + 1]
            seg = col - a_cum_t[h:h + 1, :]
            dec = jnp.exp(jnp.where(causal, seg, -jnp.inf))
            m_l.append((cb[g] * dec * dt_t[h:h + 1, :]).astype(BF16))
            ce_l.append((cm[g] * jnp.exp(col)).astype(BF16))
            bw_l.append((bm_t[g] * w_t[h:h + 1, :]).astype(BF16))
        sl = slice(j * LANES, (j + 1) * LANES)
        xs_pair = xs[:, sl]
        rhs_x = jnp.concatenate([jnp.where(lo, xs_pair, 0.0), jnp.where(lo, 0.0, xs_pair)],
                                axis=0).astype(BF16)
        st_pair = st_ref[:, sl]
        rhs_s = jnp.concatenate([jnp.where(lo, st_pair, 0.0), jnp.where(lo, 0.0, st_pair)],
                                axis=0).astype(BF16)
        lhs_y = jnp.concatenate(m_l + ce_l, axis=1)
        y_pair = jnp.dot(lhs_y, jnp.concatenate([rhs_x, rhs_s], axis=0), preferred_element_type=F32)
        st_new = jnp.dot(jnp.concatenate(bw_l, axis=1), rhs_x, preferred_element_type=F32)
        cd = jnp.exp(jnp.where(lo[0:1, :], a_last[:, 2 * j:2 * j + 1], a_last[:, 2 * j + 1:2 * j + 2]))
        st_ref[:, sl] = st_pair * cd + st_new
        y_parts.append(y_pair + dskip_ref[:, sl] * xs_pair)
        mxu_filler()

    y = jnp.concatenate(y_parts, axis=1)
    zf = proj(0)
    y = y * _silu(zf)
    gsz = d_ssm // SSD_GROUPS
    y = jnp.concatenate(
        [_rms(y[:, g * gsz:(g + 1) * gsz], gn_ref[:, g * gsz:(g + 1) * gsz]) for g in range(SSD_GROUPS)],
        axis=1)

    glu = proj(2)
    dc = glu.shape[1] // 2
    cval = glu[:, :dc] * _sigmoid(glu[:, dc:])
    hist = CONF_WIDTH - 1
    base = 32
    tile = lambda t: t * SUBLANES
    _rows_to_tiles(cbuf_ref.at[pl.ds(tile(base), tile(q)), :], cval)
    taps = ccw_ref[...].reshape(base, SUBLANES, LANES)
    tch = 16
    for t0 in range(0, q, tch):
        cacc = jnp.broadcast_to(ccb_ref[...][None], (tch, SUBLANES, LANES))
        for k in range(CONF_WIDTH):
            off = base - hist + k + t0
            window = cbuf_ref[pl.ds(tile(off), tile(tch)), :].reshape(tch, SUBLANES, LANES)
            cacc = cacc + taps[k][None] * window
        ctile_ref[pl.ds(tile(t0), tile(tch)), :] = cacc.reshape(tile(tch), LANES)
        mxu_filler()
    cbuf_ref[0:tile(base), :] = cbuf_ref[tile(q):tile(q + base), :]
    cacc = _tiles_to_rows(ctile_ref, q)
    mu = jnp.mean(cacc, axis=-1, keepdims=True)
    var = jnp.mean(jnp.square(cacc - mu), axis=-1, keepdims=True)
    cn = (cacc - mu) * lax.rsqrt(var + EPS) * lng_ref[...] + lnb_ref[...]
    cout = _silu(cn)

    h_ref[0] = (x_in + jnp.dot(y.astype(BF16), wy_ref[...], preferred_element_type=F32)
                + jnp.dot(cout.astype(BF16), wc_ref[...], preferred_element_type=F32))


def _seq(x, meta_proj, g_mix, w_in_parts, params, w_out_parts, *, n_heads):
    b, seq, d = x.shape
    xbcm, zm, glum, dtm = meta_proj
    wz, wx, wg, wd = w_in_parts
    conv_dim, d_ssm, dc = wx.shape[1], wz.shape[1], wg.shape[1] // 2
    assert dc == SUBLANES * LANES
    nc = seq // CHUNK + 1
    front_pad = (-N_META) % CHUNK
    blk = lambda n: pl.BlockSpec((1, CHUNK, n), lambda i, c: (i, jnp.maximum(c - 1, 0), 0))
    nxt = pl.BlockSpec((1, CHUNK, d), lambda i, c: (i, jnp.minimum(c, nc - 2), 0))
    full = lambda a: pl.BlockSpec(a.shape, lambda i, c: (0,) * a.ndim)
    consts = [zm, xbcm, glum, dtm, g_mix, wz, wx, wg, wd, *params, *w_out_parts]
    proj_bufs = [pltpu.VMEM((CHUNK, w.shape[1]), F32) for w in (wz, wx, wg, wd)]
    kern = functools.partial(_seq_kernel, d_ssm=d_ssm, n_heads=n_heads, front_pad=front_pad)
    return pl.pallas_call(
        kern,
        grid=(b, nc),
        in_specs=[blk(d), nxt] + [full(a) for a in consts],
        out_specs=[blk(d),
                   pl.BlockSpec((1, n_heads, SSD_HEAD_DIM, SSD_STATE), lambda i, c: (i, 0, 0, 0)),
                   pl.BlockSpec((1, SSD_CONV - 1, conv_dim), lambda i, c: (i, 0, 0)),
                   pl.BlockSpec((1, CONF_WIDTH - 1, dc), lambda i, c: (i, 0, 0))],
        out_shape=[jax.ShapeDtypeStruct((b, seq, d), F32),
                   jax.ShapeDtypeStruct((b, n_heads, SSD_HEAD_DIM, SSD_STATE), F32),
                   jax.ShapeDtypeStruct((b, SSD_CONV - 1, conv_dim), F32),
                   jax.ShapeDtypeStruct((b, CONF_WIDTH - 1, dc), F32)],
        scratch_shapes=[pltpu.VMEM((SSD_STATE, d_ssm), F32),
                        pltpu.VMEM((SUBLANES + CHUNK, conv_dim), F32),
                        pltpu.VMEM(((32 + CHUNK) * SUBLANES, LANES), F32),
                        pltpu.VMEM((CHUNK * SUBLANES, LANES), F32)] + proj_bufs + proj_bufs,
        compiler_params=_cparams(("arbitrary", "arbitrary")),
        name="seq",
    )(x, x, *consts)


def _sample_kernel(xbc_ref, z_ref, glu_ref, dt_ref, ssm_ref, shist_ref, chist_ref,
                   cw_ref, cb_ref, dtb_ref, alog_ref, dskip_ref, gn_ref,
                   ccw_ref, ccb_ref, lng_ref, lnb_ref,
                   y_ref, c_ref, ssm_o, shist_o, chist_o, *, d_ssm, n_heads):
    sb = xbc_ref.shape[0]
    n = SSD_STATE
    gw = SSD_GROUPS * n
    hp = d_ssm // SSD_GROUPS
    heads_per_group = n_heads // SSD_GROUPS

    x_new = xbc_ref[...].astype(F32)
    acc = cb_ref[...] + cw_ref[SSD_CONV - 1:SSD_CONV, :] * x_new
    for k in range(SSD_CONV - 1):
        acc = acc + cw_ref[k:k + 1, :] * shist_ref[:, k, :]
    for k in range(SSD_CONV - 2):
        shist_o[:, k, :] = shist_ref[:, k + 1, :]
    shist_o[:, SSD_CONV - 2, :] = x_new
    xact = _silu(acc)
    xs = xact[:, :d_ssm]
    dtv = _softplus(dt_ref[...] + dtb_ref[...])
    a = -jnp.exp(alog_ref[...])
    decay = jnp.exp(dtv * a)

    hsel = (lax.broadcasted_iota(jnp.int32, (LANES, d_ssm), 1) // SSD_HEAD_DIM
            == lax.broadcasted_iota(jnp.int32, (LANES, d_ssm), 0)).astype(F32)
    dt_x = jnp.dot(dtv, hsel, preferred_element_type=F32, precision=HIGHEST)
    xdt = (xs * dt_x).astype(BF16)
    rowid = lax.broadcasted_iota(jnp.int32, (sb, 1), 0)

    y_rows = []
    for i in range(sb):
        sel = rowid == i
        xi = jnp.where(sel, xdt, jnp.zeros_like(xdt))
        parts = []
        for g in range(SSD_GROUPS):
            bm = xact[:, d_ssm + g * n:d_ssm + (g + 1) * n].astype(BF16)
            cmat = xact[:, d_ssm + gw + g * n:d_ssm + gw + (g + 1) * n].astype(BF16)
            outer = lax.dot_general(xi[:, g * hp:(g + 1) * hp], bm, (((0,), (0,)), ((), ())),
                                    preferred_element_type=F32)
            outer = outer.reshape(heads_per_group, SSD_HEAD_DIM, n)
            hs = slice(g * heads_per_group, (g + 1) * heads_per_group)
            s_old = ssm_ref[i, hs]
            dec_i = jnp.stack([jnp.broadcast_to(decay[i:i + 1, h:h + 1], (SSD_HEAD_DIM, n))
                               for h in range(g * heads_per_group, (g + 1) * heads_per_group)])
            s_new = s_old * dec_i + outer
            ssm_o[i, hs] = s_new
            yv = lax.dot_general(cmat, s_new.reshape(hp, n).astype(BF16), (((1,), (1,)), ((), ())),
                                 preferred_element_type=F32)
            parts.append(yv)
        yfull = jnp.concatenate(parts, axis=1)
        y_rows.append(jnp.where(sel, yfull, 0.0))
    y = y_rows[0]
    for r in y_rows[1:]:
        y = y + r
    y = y + dskip_ref[...] * xs
    y = y * _silu(z_ref[...].astype(F32))
    gsz = d_ssm // SSD_GROUPS
    y = jnp.concatenate(
        [_rms(y[:, g * gsz:(g + 1) * gsz], gn_ref[:, g * gsz:(g + 1) * gsz]) for g in range(SSD_GROUPS)],
        axis=1)
    y_ref[...] = y.astype(BF16)

    glu = glu_ref[...].astype(F32)
    dc = glu.shape[1] // 2
    cval = glu[:, :dc] * _sigmoid(glu[:, dc:])
    hist = CONF_WIDTH - 1
    cacc = ccb_ref[...] + ccw_ref[hist:hist + 1, :] * cval
    cacc = cacc + jnp.sum(chist_ref[...] * ccw_ref[0:hist, :][None], axis=1)
    chist_o[:, 0:hist - 1, :] = chist_ref[:, 1:hist, :]
    chist_o[:, hist - 1, :] = cval
    mu = jnp.mean(cacc, axis=-1, keepdims=True)
    var = jnp.mean(jnp.square(cacc - mu), axis=-1, keepdims=True)
    cn = (cacc - mu) * lax.rsqrt(var + EPS) * lng_ref[...] + lnb_ref[...]
    c_ref[...] = _silu(cn).astype(BF16)


def _sample(xbc, z, glu, dt, ssm, shist, chist, params, *, n_heads):
    bs, conv_dim = xbc.shape
    d_ssm = z.shape[1]
    dc = glu.shape[1] // 2
    sb = SUBLANES
    assert bs % sb == 0
    row = lambda n: pl.BlockSpec((sb, n), lambda i: (i, 0))
    full = lambda a: pl.BlockSpec(a.shape, lambda i: (0,) * a.ndim)
    st_spec = pl.BlockSpec((sb, n_heads, SSD_HEAD_DIM, SSD_STATE), lambda i: (i, 0, 0, 0))
    sh_spec = pl.BlockSpec((sb, SSD_CONV - 1, conv_dim), lambda i: (i, 0, 0))
    ch_spec = pl.BlockSpec((sb, CONF_WIDTH - 1, dc), lambda i: (i, 0, 0))
    kern = functools.partial(_sample_kernel, d_ssm=d_ssm, n_heads=n_heads)
    return pl.pallas_call(
        kern,
        grid=(bs // sb,),
        in_specs=[row(conv_dim), row(d_ssm), row(2 * dc), row(LANES), st_spec, sh_spec, ch_spec]
        + [full(p) for p in params],
        out_specs=[row(d_ssm), row(dc), st_spec, sh_spec, ch_spec],
        out_shape=[jax.ShapeDtypeStruct((bs, d_ssm), BF16),
                   jax.ShapeDtypeStruct((bs, dc), BF16),
                   jax.ShapeDtypeStruct(ssm.shape, F32),
                   jax.ShapeDtypeStruct(shist.shape, F32),
                   jax.ShapeDtypeStruct(chist.shape, F32)],
        compiler_params=_cparams(("parallel",)),
        name="sample_step",
    )(xbc, z, glu, dt, ssm, shist, chist, *params)


def _router_kernel(*refs, with_proj):
    if with_proj:
        y_ref, c_ref, res_ref, wy_ref, wc_ref = refs[:5]
        g_ref, wr_ref, br_ref, cnt_in_ref, _, h_ref = refs[5:11]
        xn_ref, idx_ref, gate_ref, rank_ref, cnt_ref, carry_ref = refs[11:]
        h = (res_ref[...]
             + jnp.dot(y_ref[...], wy_ref[...], preferred_element_type=F32)
             + jnp.dot(c_ref[...], wc_ref[...], preferred_element_type=F32))
        h_ref[...] = h
    else:
        h_in_ref, g_ref, wr_ref, br_ref, cnt_in_ref, _ = refs[:6]
        xn_ref, idx_ref, gate_ref, rank_ref, cnt_ref, carry_ref = refs[6:]
        h = h_in_ref[...]
    i = pl.program_id(0)

    @pl.when(i == 0)
    def _():
        carry_ref[...] = cnt_in_ref[...]

    xn = _rms(h, g_ref[...])
    _rows_to_tiles(xn_ref, xn)
    tm = h.shape[0]
    logits = lax.dot_general(wr_ref[...], xn, (((1,), (1,)), ((), ())),
                             preferred_element_type=F32, precision=HIGHEST) + br_ref[...]
    eid = lax.broadcasted_iota(jnp.int32, logits.shape, 0)
    vals, idxs, hots = [], [], []
    work = logits
    for _ in range(TOP_K):
        m = jnp.max(work, axis=0, keepdims=True)
        sel = jnp.min(jnp.where(work == m, eid, N_EXPERTS), axis=0, keepdims=True)
        hot = eid == sel
        work = jnp.where(hot, -jnp.inf, work)
        vals.append(m)
        idxs.append(sel)
        hots.append(hot)
    exps = [jnp.exp(v - vals[0]) for v in vals]
    den = exps[0]
    for e in exps[1:]:
        den = den + e
    chosen = hots[0]
    for hot in hots[1:]:
        chosen = jnp.logical_or(chosen, hot)
    chosen_f = chosen.astype(F32)
    r = lax.broadcasted_iota(jnp.int32, (tm, tm), 0)
    cidx = lax.broadcasted_iota(jnp.int32, (tm, tm), 1)
    upper = (r < cidx).astype(BF16)
    cum = jnp.dot(chosen_f.astype(BF16), upper, preferred_element_type=F32) + carry_ref[:, 0:1]
    for k in range(TOP_K):
        idx_ref[k:k + 1, :] = idxs[k]
        gate_ref[k:k + 1, :] = exps[k] / den
        rank_ref[k:k + 1, :] = jnp.sum(jnp.where(hots[k], cum, 0.0), axis=0, keepdims=True).astype(jnp.int32)
    carry_ref[...] = carry_ref[...] + jnp.sum(chosen_f, axis=1, keepdims=True)
    cnt_ref[...] = carry_ref[...]


def _router(h_or_proj, g, wr_t, br, cnt_in, xn_buf, row_off):
    with_proj = isinstance(h_or_proj, tuple)
    t, d = (h_or_proj[2] if with_proj else h_or_proj).shape
    assert d == SUBLANES * LANES
    tm = min(512, t)
    assert t % tm == 0 and row_off % tm == 0
    off = row_off // tm
    row = lambda n: pl.BlockSpec((tm, n), lambda i: (i, 0))
    col = pl.BlockSpec((TOP_K, tm), lambda i: (0, i))
    full = lambda a: pl.BlockSpec(a.shape, lambda i: (0,) * a.ndim)
    if with_proj:
        y, c, res, wy, wc = h_or_proj
        args = [y, c, res, wy, wc]
        in_specs = [row(y.shape[1]), row(c.shape[1]), row(d), full(wy), full(wc)]
        out_specs, out_shape = [row(d)], [jax.ShapeDtypeStruct((t, d), F32)]
    else:
        args, in_specs, out_specs, out_shape = [h_or_proj], [row(d)], [], []
    args += [g, wr_t, br, cnt_in, xn_buf]
    in_specs += [full(g), full(wr_t), full(br), full(cnt_in), pl.BlockSpec(memory_space=pl.ANY)]
    out_specs += [pl.BlockSpec((tm * SUBLANES, LANES), lambda i: (i + off, 0)), col, col, col, full(cnt_in)]
    out_shape += [jax.ShapeDtypeStruct(xn_buf.shape, F32),
                  jax.ShapeDtypeStruct((TOP_K, t), jnp.int32),
                  jax.ShapeDtypeStruct((TOP_K, t), F32),
                  jax.ShapeDtypeStruct((TOP_K, t), jnp.int32),
                  jax.ShapeDtypeStruct(cnt_in.shape, F32)]
    return pl.pallas_call(
        functools.partial(_router_kernel, with_proj=with_proj),
        grid=(t // tm,),
        in_specs=in_specs,
        out_specs=out_specs,
        out_shape=out_shape,
        scratch_shapes=[pltpu.VMEM(cnt_in.shape, F32)],
        input_output_aliases={len(args) - 1: len(out_shape) - 5},
        compiler_params=_cparams(("arbitrary",)),
        name="router",
    )(*args)


_INV_GROUP = 16


def _inverse_kernel(pad_lo_ref, pad_hi_ref, dest_ref, inv_ref, *, pad_base, fill_groups):
    step = pl.program_id(0)

    @pl.when(step == 0)
    def _():
        def sentinel(d, carry):
            inv_ref[d] = (pad_base + jnp.bitwise_and(d, MOE_BLOCK - 1)) * TOP_K
            return carry

        for e in range(pad_lo_ref.shape[0]):
            lax.fori_loop(pad_lo_ref[e], pad_hi_ref[e], sentinel, 0)

    @pl.when(step > 0)
    def _():
        def fill(jj, carry):
            a0 = ((step - 1) * fill_groups + jj) * _INV_GROUP
            for c in range(_INV_GROUP):
                inv_ref[dest_ref[a0 + c]] = a0 + c
            return carry

        lax.fori_loop(0, fill_groups, fill, 0)


def _build_inverse(dest, pad_lo, pad_hi, n_rows, pad_base):
    n_fill = dest.shape[0] // _INV_GROUP
    assert dest.shape[0] % _INV_GROUP == 0
    slices = max(g for g in range(1, 17) if n_fill % g == 0)
    kern = functools.partial(_inverse_kernel, pad_base=pad_base, fill_groups=n_fill // slices)
    smem = pl.BlockSpec(memory_space=pltpu.SMEM)
    return pl.pallas_call(
        kern,
        grid=(1 + slices,),
        in_specs=[smem, smem, smem],
        out_specs=smem,
        out_shape=jax.ShapeDtypeStruct((n_rows,), jnp.int32),
        compiler_params=_cparams(("arbitrary",)),
        name="inverse_table",
    )(pad_lo, pad_hi, dest)


def _moe_kernel(be_ref, na_ref, tok_ref, row_ref, xn_hbm, wg_ref, bg_ref, wu_ref, bu_ref, wd_ref, bd_ref,
                yk_hbm, wg_s, wu_s, wd_s, x0, x1, o0, o1, gsem, ssem, *, trash_row):
    i = pl.program_id(0)
    n_active = na_ref[0]
    tab_rows = na_ref[1]
    last_block = pl.num_programs(0) - 2
    xbuf, obuf = (x0, x1), (o0, o1)
    prev = be_ref[jnp.maximum(i - 1, 0)]
    new_expert = jnp.logical_or(i == 0, be_ref[i] != prev)
    active = i < n_active

    def gather_start(block, slot):
        @pl.when(tab_rows > 0)
        def _():
            for r in range(MOE_BLOCK):
                src = pl.multiple_of(tok_ref[block * (MOE_BLOCK // LANES) + r // LANES, r % LANES], SUBLANES)
                pltpu.make_async_copy(xn_hbm.at[pl.ds(src, SUBLANES), :],
                                      xbuf[slot].at[pl.ds(r * SUBLANES, SUBLANES), :],
                                      gsem.at[slot]).start(priority=r % 2)

    def gather_wait(slot):
        pltpu.make_async_copy(xn_hbm.at[pl.ds(0, MOE_BLOCK * SUBLANES), :], xbuf[slot],
                              gsem.at[slot]).wait()

    def scatter_start(block, slot, real, trash):
        for r in range(MOE_BLOCK):
            dst = jnp.where(real, row_ref[block * (MOE_BLOCK // LANES) + r // LANES, r % LANES],
                            (trash + r) * SUBLANES)
            pltpu.make_async_copy(obuf[slot].at[pl.ds(r * SUBLANES, SUBLANES), :],
                                  yk_hbm.at[pl.ds(pl.multiple_of(dst, SUBLANES), SUBLANES), :],
                                  ssem.at[slot]).start(priority=r % 2)

    def scatter_wait(slot):
        pltpu.make_async_copy(obuf[slot], yk_hbm.at[pl.ds(0, MOE_BLOCK * SUBLANES), :],
                              ssem.at[slot]).wait()

    @pl.when(i == 0)
    def _():
        o0[...] = jnp.zeros_like(o0)
        o1[...] = jnp.zeros_like(o1)
        gather_start(0, 0)
        scatter_start(0, 0, False, trash_row)

    @pl.when(jnp.logical_and(active, new_expert))
    def _():
        wg_s[...] = wg_ref[0].astype(BF16)
        wu_s[...] = wu_ref[0].astype(BF16)
        wd_s[...] = wd_ref[0].astype(BF16)

    for p in (0, 1):
        mine = (i % 2) == p

        @pl.when(jnp.logical_and(active, mine))
        def _():
            gather_wait(p)
            gather_start(jnp.minimum(i + 1, last_block), 1 - p)

        @pl.when(jnp.logical_and(active, mine))
        def _():
            scatter_start(jnp.maximum(i - 1, 0), 1 - p, i > 0, trash_row + MOE_BLOCK)
            x = _tiles_to_rows(xbuf[p], MOE_BLOCK).astype(BF16)
            gt = jnp.minimum(jnp.dot(x, wg_s[...], preferred_element_type=F32) + bg_ref[0], SWIGLU_LIMIT)
            up = jnp.clip(jnp.dot(x, wu_s[...], preferred_element_type=F32) + bu_ref[0],
                          -SWIGLU_LIMIT, SWIGLU_LIMIT)
            hdn = (up + 1.0) * gt * _sigmoid(gt * SWIGLU_ALPHA)
            y = jnp.dot(hdn.astype(BF16), wd_s[...], preferred_element_type=F32) + bd_ref[0]
            scatter_wait(p)
            _rows_to_tiles(obuf[p], y)

        @pl.when(jnp.logical_and(i == n_active, mine))
        def _():
            gather_wait(p)
            scatter_wait(p)
            scatter_start(i - 1, 1 - p, True, trash_row)
            scatter_wait(1 - p)


def _moe(block_expert, n_active, tok2d, row2d, xn, w_gate, b_gate, w_up, b_up, w_down, b_down, *,
         yk_rows, trash_row):
    d = w_gate.shape[1]
    assert d == SUBLANES * LANES and xn.shape[1] == LANES
    nb = tok2d.shape[0] * LANES // MOE_BLOCK
    dff = w_gate.shape[2]
    wspec = lambda a: pl.BlockSpec((1,) + a.shape[1:], lambda i, be, na, tk, rw: (be[i], 0, 0))
    blk = pltpu.VMEM((MOE_BLOCK * SUBLANES, LANES), F32)
    kern = functools.partial(_moe_kernel, trash_row=trash_row)
    return pl.pallas_call(
        kern,
        grid_spec=pltpu.PrefetchScalarGridSpec(
            num_scalar_prefetch=4,
            grid=(nb + 1,),
            in_specs=[pl.BlockSpec(memory_space=pl.ANY), wspec(w_gate), wspec(b_gate), wspec(w_up),
                      wspec(b_up), wspec(w_down), wspec(b_down)],
            out_specs=pl.BlockSpec(memory_space=pl.ANY),
            scratch_shapes=[pltpu.VMEM((d, dff), BF16), pltpu.VMEM((d, dff), BF16),
                            pltpu.VMEM((dff, d), BF16), blk, blk, blk, blk,
                            pltpu.SemaphoreType.DMA((2,)), pltpu.SemaphoreType.DMA((2,))],
        ),
        out_shape=jax.ShapeDtypeStruct((yk_rows * SUBLANES, LANES), F32),
        compiler_params=_cparams(("arbitrary",)),
        name="moe_experts",
    )(block_expert, n_active, tok2d, row2d, xn, w_gate, b_gate, w_up, b_up, w_down, b_down)


def _combine_kernel(h_ref, gate_ref, gf_ref, *rest):
    yk_refs, o_ref = rest[:TOP_K], rest[TOP_K]
    acc = h_ref[...]
    tm = acc.shape[0]
    for k in range(TOP_K):
        acc = acc + gate_ref[:, k:k + 1] * _tiles_to_rows(yk_refs[k], tm)
    o_ref[...] = _rms(acc, gf_ref[...])


def _combine(h, gates, g_final, yk, row_off, plane_tokens):
    t, d = h.shape
    tm = min(512, t)
    assert t % tm == 0
    plane = lambda k: pl.BlockSpec(
        (pl.Element(tm * SUBLANES), pl.Element(LANES)),
        lambda i: ((k * plane_tokens + row_off + i * tm) * SUBLANES, 0))
    return pl.pallas_call(
        _combine_kernel,
        grid=(t // tm,),
        in_specs=[pl.BlockSpec((tm, d), lambda i: (i, 0)),
                  pl.BlockSpec((tm, TOP_K), lambda i: (i, 0)),
                  pl.BlockSpec(g_final.shape, lambda i: (0, 0))] + [plane(k) for k in range(TOP_K)],
        out_specs=pl.BlockSpec((tm, d), lambda i: (i, 0)),
        out_shape=jax.ShapeDtypeStruct((t, d), F32),
        compiler_params=_cparams(("parallel",)),
        name="combine",
    )(h, gates, g_final, *([yk] * TOP_K))


def _pad_lanes(v, fill=0.0):
    v = v.reshape(1, -1).astype(F32)
    return jnp.pad(v, ((0, 0), (0, LANES - v.shape[1])), constant_values=fill)


def kernel(x_prompt, x_sample, state_ssm, state_ssd_conv, state_conf_conv, meta_tokens, g_mix, w_in,
           conv_ssd_w, conv_ssd_b, dt_bias, a_log, d_skip, g_ssd_norm, conv_conf_w, conv_conf_b,
           ln_conf_g, ln_conf_b, w_out, g_ffn, w_router, b_router, w_gate, b_gate, w_up, b_up,
           w_down, b_down, g_final):
    bp, seq, d = x_prompt.shape
    bs = x_sample.shape[0]
    depth = w_in.shape[0]
    assert depth == 1 and x_sample.shape[1] == 1 and seq % CHUNK == 0
    n_heads = a_log.shape[1]
    d_ssm = n_heads * SSD_HEAD_DIM
    conv_dim = conv_ssd_w.shape[2]
    dc = conv_conf_w.shape[2]
    l = 0

    o1, o2, o3 = d_ssm, d_ssm + conv_dim, d_ssm + conv_dim + n_heads
    w_in_l = w_in[l]
    wz = w_in_l[:, :o1].astype(BF16)
    wx = w_in_l[:, o1:o2].astype(BF16)
    wdt = jnp.pad(w_in_l[:, o2:o3], ((0, 0), (0, LANES - n_heads))).astype(BF16)
    wglu = w_in_l[:, o3:].astype(BF16)
    row2 = lambda v: v.reshape(1, -1).astype(F32)
    seq_params = (
        conv_ssd_w[l], row2(conv_ssd_b[l]), _pad_lanes(dt_bias[l]), _pad_lanes(a_log[l]),
        row2(jnp.repeat(d_skip[l], SSD_HEAD_DIM)), row2(g_ssd_norm[l]),
        jnp.pad(conv_conf_w[l], ((0, 32 - CONF_WIDTH), (0, 0))), row2(conv_conf_b[l]),
        row2(ln_conf_g[l]), row2(ln_conf_b[l]))
    g_mix_r = row2(g_mix[l])
    w_out_l = w_out[l]
    wy = w_out_l[:d_ssm].astype(BF16)
    wc = w_out_l[d_ssm:].astype(BF16)
    prompt_params = seq_params[:6] + (
        seq_params[6].reshape(32 * SUBLANES, LANES), conv_conf_b[l].reshape(SUBLANES, LANES).astype(F32)
    ) + seq_params[8:]

    front_pad = (-N_META) % CHUNK
    small = jnp.concatenate([jnp.zeros((front_pad, d), F32), meta_tokens.astype(F32),
                             x_sample.reshape(bs, d)], axis=0)
    z_m, xbc_m, glu_m, dt_m = _inproj(small, g_mix_r, wz, wx, wglu, wdt)

    h_p, ssm_p, shist_p, chist_p = _seq(
        x_prompt, (xbc_m[:CHUNK], z_m[:CHUNK], glu_m[:CHUNK], dt_m[:CHUNK]), g_mix_r,
        (wz, wx, wglu, wdt), prompt_params, (wy, wc), n_heads=n_heads)

    y_s, c_s, ssm_s, shist_s, chist_s = _sample(
        xbc_m[CHUNK:], z_m[CHUNK:], glu_m[CHUNK:], dt_m[CHUNK:],
        state_ssm[l], state_ssd_conv[l], state_conf_conv[l], seq_params, n_heads=n_heads)

    g_ffn_r = row2(g_ffn[l])
    wr_t = w_router[l].T.astype(F32)
    br = b_router[l].reshape(N_EXPERTS, 1).astype(F32)
    cnt0 = jnp.zeros((N_EXPERTS, LANES), F32)
    t_p = bp * seq
    t_all = t_p + bs
    assert (t_all * TOP_K) % LANES == 0
    h_p = h_p.reshape(t_p, d)
    xn_all = jnp.zeros((t_all * SUBLANES, LANES), F32)
    xn_all, idx_p, gate_p, rank_p, cnt1 = _router(h_p, g_ffn_r, wr_t, br, cnt0, xn_all, 0)
    h_s, xn_all, idx_s, gate_s, rank_s, cnt2 = _router(
        (y_s, c_s, x_sample.reshape(bs, d), wy, wc), g_ffn_r, wr_t, br, cnt1, xn_all, t_p)

    counts = cnt2[:, 0].astype(jnp.int32)
    padded = (counts + MOE_BLOCK - 1) // MOE_BLOCK * MOE_BLOCK
    pad_end = jnp.cumsum(padded)
    pad_start = pad_end - padded
    n_blocks = -(-(t_all * TOP_K) // MOE_BLOCK) + N_EXPERTS
    rows = n_blocks * MOE_BLOCK
    blk_first = jnp.arange(n_blocks + 1, dtype=jnp.int32) * MOE_BLOCK
    block_expert = jnp.minimum(jnp.sum(pad_end[None, :] <= blk_first[:, None], axis=1),
                               N_EXPERTS - 1).astype(jnp.int32)
    n_active = jnp.stack([pad_end[-1] // MOE_BLOCK,
                          jnp.asarray(MOE_BLOCK // LANES, jnp.int32)]).astype(jnp.int32)
    idx_all = jnp.concatenate([idx_p, idx_s], axis=1)
    rank_all = jnp.concatenate([rank_p, rank_s], axis=1)
    experts = jnp.arange(N_EXPERTS, dtype=jnp.int32)[:, None, None]
    start_of = jnp.sum(jnp.where(idx_all[None] == experts, pad_start[:, None, None], 0), axis=0)
    dest = (start_of + rank_all).T.reshape(-1).astype(jnp.int32)

    trash = TOP_K * t_all
    pad_lo = jnp.concatenate([pad_start + counts, pad_end[-1:]]).astype(jnp.int32)
    pad_hi = jnp.concatenate([pad_end, jnp.full((1,), rows, jnp.int32)]).astype(jnp.int32)
    inv = _build_inverse(dest, pad_lo, pad_hi, rows, t_all).reshape(rows // LANES, LANES)
    inv_tok = inv // TOP_K
    tok2d = jnp.minimum(inv_tok, t_all - 1) * SUBLANES
    row2d = jnp.where(inv_tok < t_all, (inv % TOP_K) * t_all + inv_tok, trash + inv_tok - t_all) * SUBLANES
    yk = _moe(block_expert, n_active, tok2d, row2d, xn_all,
              w_gate[l], b_gate[l].reshape(N_EXPERTS, 1, -1), w_up[l], b_up[l].reshape(N_EXPERTS, 1, -1),
              w_down[l], b_down[l].reshape(N_EXPERTS, 1, -1), yk_rows=trash + 2 * MOE_BLOCK, trash_row=trash)
    g_fin = row2(g_final)
    y_prompt = _combine(h_p, gate_p.T, g_fin, yk, 0, t_all).reshape(bp, seq, d)
    y_sample = _combine(h_s, gate_s.T, g_fin, yk, t_p, t_all).reshape(bs, 1, d)

    return (y_prompt, y_sample, ssm_p[None], shist_p[None], chist_p[None],
            ssm_s[None], shist_s[None], chist_s[None])
```

```python
import functools

import jax
import jax.numpy as jnp
from jax import lax
from jax.experimental import pallas as pl
from jax.experimental.pallas import tpu as pltpu

F32 = jnp.float32
BF16 = jnp.bfloat16
HIGHEST = lax.Precision.HIGHEST

N_META = 16
SSD_HEAD_DIM = 64
SSD_GROUPS = 2
SSD_STATE = 128
SSD_CONV = 4
CHUNK = 128
CONF_WIDTH = 31
N_EXPERTS = 32
TOP_K = 4
SWIGLU_LIMIT = 7.0
SWIGLU_ALPHA = 1.702
EPS = 1e-5

LANES = 128
LANE_BITS = LANES.bit_length() - 1
MXU_COLS = 256
SUBLANES = 8
MOE_BLOCK = 256
VMEM_LIMIT = 56 * 1024 * 1024


def _cparams(sem):
    return pltpu.CompilerParams(dimension_semantics=sem, vmem_limit_bytes=VMEM_LIMIT)


def _sigmoid(x):
    return 1.0 / (1.0 + jnp.exp(-x))


def _tiles_to_rows(ref, n):
    return jnp.concatenate([ref[pl.ds(s, n, stride=SUBLANES), :] for s in range(SUBLANES)], axis=1)


def _rows_to_tiles(ref, val):
    n = val.shape[0]
    for s in range(SUBLANES):
        ref[pl.ds(s, n, stride=SUBLANES), :] = val[:, s * LANES:(s + 1) * LANES]


def _silu(x):
    return x * _sigmoid(x)


def _softplus(x):
    return jnp.maximum(x, 0.0) + jnp.log1p(jnp.exp(-jnp.abs(x)))


def _rms(x, g):
    return x * lax.rsqrt(jnp.mean(x * x, axis=-1, keepdims=True) + EPS) * g


def _inproj_kernel(x_ref, g_ref, wz_ref, wx_ref, wg_ref, wd_ref, z_ref, xbc_ref, glu_ref, dt_ref):
    u = _rms(x_ref[...], g_ref[...]).astype(BF16)
    z_ref[...] = jnp.dot(u, wz_ref[...], preferred_element_type=F32).astype(BF16)
    xbc_ref[...] = jnp.dot(u, wx_ref[...], preferred_element_type=F32).astype(BF16)
    glu_ref[...] = jnp.dot(u, wg_ref[...], preferred_element_type=F32).astype(BF16)
    dt_ref[...] = jnp.dot(u, wd_ref[...], preferred_element_type=F32)


def _inproj(x, g, wz, wx, wg, wd):
    t, d = x.shape
    tm = min(512, t)
    assert t % tm == 0
    row = lambda n: pl.BlockSpec((tm, n), lambda i: (i, 0))
    full = lambda a: pl.BlockSpec(a.shape, lambda i: (0, 0))
    return pl.pallas_call(
        _inproj_kernel,
        grid=(t // tm,),
        in_specs=[row(d), full(g), full(wz), full(wx), full(wg), full(wd)],
        out_specs=[row(wz.shape[1]), row(wx.shape[1]), row(wg.shape[1]), row(wd.shape[1])],
        out_shape=[jax.ShapeDtypeStruct((t, wz.shape[1]), BF16),
                   jax.ShapeDtypeStruct((t, wx.shape[1]), BF16),
                   jax.ShapeDtypeStruct((t, wg.shape[1]), BF16),
                   jax.ShapeDtypeStruct((t, wd.shape[1]), F32)],
        compiler_params=_cparams(("parallel",)),
        name="inproj",
    )(x, g, wz, wx, wg, wd)


def _seq_kernel(x_ref, xnext_ref, zm_ref, xbcm_ref, glum_ref, dtm_ref, gmix_ref,
                wz_ref, wx_ref, wg_ref, wd_ref, *rest, **static):
    c = pl.program_id(1)
    bufs = (rest[-8:-4], rest[-4:])
    meta = (zm_ref, xbcm_ref, glum_ref, dtm_ref)
    st_ref, xbuf_ref, cbuf_ref = rest[-12:-9]

    @pl.when(c == 0)
    def _():
        st_ref[...] = jnp.zeros_like(st_ref)
        xbuf_ref[...] = jnp.zeros_like(xbuf_ref)
        cbuf_ref[...] = jnp.zeros_like(cbuf_ref)

    for parity in (0, 1):

        @pl.when(c % 2 == parity)
        def _():
            cur, nxt = bufs[parity], bufs[1 - parity]
            proj = lambda s: jnp.where(c == 0, meta[s][...].astype(F32), cur[s][...])
            u_next = _rms(xnext_ref[0], gmix_ref[...]).astype(BF16)
            tiles = [(dst, w_ref, n0) for dst, w_ref in zip(nxt, (wz_ref, wx_ref, wg_ref, wd_ref))
                     for n0 in range(0, w_ref.shape[1], MXU_COLS)]
            pending = iter(tiles)

            def next_tile():
                item = next(pending, None)
                if item is not None:
                    dst, w_ref, n0 = item
                    n1 = min(n0 + MXU_COLS, w_ref.shape[1])
                    dst[:, n0:n1] = jnp.dot(u_next, w_ref[:, n0:n1], preferred_element_type=F32)

            _seq_chunk(proj, next_tile, x_ref, *rest[:-8], **static)
            for _ in tiles:
                next_tile()

    @pl.when(c == pl.num_programs(1) - 1)
    def _():
        ssm_ref, shist_ref, chist_ref = rest[-15:-12]
        hist = CONF_WIDTH - 1
        ssm_ref[0] = st_ref[...].T.reshape(static["n_heads"], SSD_HEAD_DIM, SSD_STATE)
        shist_ref[0] = xbuf_ref[SUBLANES - (SSD_CONV - 1):SUBLANES, :]
        chist_ref[0] = _tiles_to_rows(cbuf_ref.at[pl.ds((32 - hist) * SUBLANES, hist * SUBLANES), :], hist)


def _seq_chunk(proj, mxu_filler, x_ref,
               cw_ref, cb_ref, dtb_ref, alog_ref, dskip_ref, gn_ref,
               ccw_ref, ccb_ref, lng_ref, lnb_ref, wy_ref, wc_ref,
               h_ref, ssm_ref, shist_ref, chist_ref,
               st_ref, xbuf_ref, cbuf_ref, ctile_ref, *, d_ssm, n_heads, front_pad):
    c = pl.program_id(1)
    last = pl.num_programs(1) - 1
    is_meta = c == 0
    q = CHUNK
    n = SSD_STATE
    gw = SSD_GROUPS * n
    x_in = x_ref[0]

    xbc_raw = proj(1)
    xbuf_ref[SUBLANES:SUBLANES + q, :] = xbc_raw
    acc = cb_ref[...] + cw_ref[SSD_CONV - 1:SSD_CONV, :] * xbc_raw
    for j in range(1, SSD_CONV):
        mxu_filler()
        acc = acc + cw_ref[SSD_CONV - 1 - j:SSD_CONV - j, :] * xbuf_ref[SUBLANES - j:SUBLANES - j + q, :]
    xbuf_ref[0:SUBLANES, :] = xbuf_ref[q:q + SUBLANES, :]
    xact = _silu(acc)
    xs = xact[:, :d_ssm]
    bm = [xact[:, d_ssm + g * n:d_ssm + (g + 1) * n] for g in range(SSD_GROUPS)]
    cm = [xact[:, d_ssm + gw + g * n:d_ssm + gw + (g + 1) * n] for g in range(SSD_GROUPS)]

    dt_raw = proj(3)
    rows = lax.broadcasted_iota(jnp.int32, (q, LANES), 0)
    cols = lax.broadcasted_iota(jnp.int32, (q, LANES), 1)
    valid = jnp.logical_or(jnp.logical_not(is_meta), rows >= front_pad)
    dtv = jnp.where(valid, _softplus(dt_raw + dtb_ref[...]), 0.0)
    da = dtv * (-jnp.exp(alog_ref[...]))
    causal = rows >= cols
    tri = causal.astype(F32)
    a_cum = jnp.dot(tri, da, preferred_element_type=F32, precision=HIGHEST)
    a_cum_t = a_cum.T
    dt_t = dtv.T
    w_t = jnp.exp(a_cum_t[:, q - 1:q] - a_cum_t) * dt_t
    a_last = a_cum[q - 1:q, :]

    bm_t = [b.T for b in bm]
    cb = [lax.dot_general(cm[g].astype(BF16), bm[g].astype(BF16), (((1,), (1,)), ((), ())),
                          preferred_element_type=F32) for g in range(SSD_GROUPS)]
    lane = lax.broadcasted_iota(jnp.int32, (q, LANES), 1)
    lo = lane < SSD_HEAD_DIM
    heads_per_group = n_heads // SSD_GROUPS

    y_parts = []
    for j in range(n_heads // 2):
        g = (2 * j) // heads_per_group
        m_l, ce_l, bw_l = [], [], []
        for h in (2 * j, 2 * j + 1):
            col = a_cum[:, h:h + 1]
            seg = col - a_cum_t[h:h + 1, :]
            dec = jnp.exp(jnp.where(causal, seg, -jnp.inf))
            m_l.append((cb[g] * dec * dt_t[h:h + 1, :]).astype(BF16))
            ce_l.append((cm[g] * jnp.exp(col)).astype(BF16))
            bw_l.append((bm_t[g] * w_t[h:h + 1, :]).astype(BF16))
        sl = slice(j * LANES, (j + 1) * LANES)
        xs_pair = xs[:, sl]
        rhs_x = jnp.concatenate([jnp.where(lo, xs_pair, 0.0), jnp.where(lo, 0.0, xs_pair)],
                                axis=0).astype(BF16)
        st_pair = st_ref[:, sl]
        rhs_s = jnp.concatenate([jnp.where(lo, st_pair, 0.0), jnp.where(lo, 0.0, st_pair)],
                                axis=0).astype(BF16)
        lhs_y = jnp.concatenate(m_l + ce_l, axis=1)
        y_pair = jnp.dot(lhs_y, jnp.concatenate([rhs_x, rhs_s], axis=0), preferred_element_type=F32)
        st_new = jnp.dot(jnp.concatenate(bw_l, axis=1), rhs_x, preferred_element_type=F32)
        cd = jnp.exp(jnp.where(lo[0:1, :], a_last[:, 2 * j:2 * j + 1], a_last[:, 2 * j + 1:2 * j + 2]))
        st_ref[:, sl] = st_pair * cd + st_new
        y_parts.append(y_pair + dskip_ref[:, sl] * xs_pair)
        mxu_filler()

    y = jnp.concatenate(y_parts, axis=1)
    zf = proj(0)
    y = y * _silu(zf)
    gsz = d_ssm // SSD_GROUPS
    y = jnp.concatenate(
        [_rms(y[:, g * gsz:(g + 1) * gsz], gn_ref[:, g * gsz:(g + 1) * gsz]) for g in range(SSD_GROUPS)],
        axis=1)

    glu = proj(2)
    dc = glu.shape[1] // 2
    cval = glu[:, :dc] * _sigmoid(glu[:, dc:])
    hist = CONF_WIDTH - 1
    base = 32
    tile = lambda t: t * SUBLANES
    _rows_to_tiles(cbuf_ref.at[pl.ds(tile(base), tile(q)), :], cval)
    taps = ccw_ref[...].reshape(base, SUBLANES, LANES)
    tch = 16
    for t0 in range(0, q, tch):
        cacc = jnp.broadcast_to(ccb_ref[...][None], (tch, SUBLANES, LANES))
        for k in range(CONF_WIDTH):
            off = base - hist + k + t0
            window = cbuf_ref[pl.ds(tile(off), tile(tch)), :].reshape(tch, SUBLANES, LANES)
            cacc = cacc + taps[k][None] * window
        ctile_ref[pl.ds(tile(t0), tile(tch)), :] = cacc.reshape(tile(tch), LANES)
        mxu_filler()
    cbuf_ref[0:tile(base), :] = cbuf_ref[tile(q):tile(q + base), :]
    cacc = _tiles_to_rows(ctile_ref, q)
    mu = jnp.mean(cacc, axis=-1, keepdims=True)
    var = jnp.mean(jnp.square(cacc - mu), axis=-1, keepdims=True)
    cn = (cacc - mu) * lax.rsqrt(var + EPS) * lng_ref[...] + lnb_ref[...]
    cout = _silu(cn)

    h_ref[0] = (x_in + jnp.dot(y.astype(BF16), wy_ref[...], preferred_element_type=F32)
                + jnp.dot(cout.astype(BF16), wc_ref[...], preferred_element_type=F32))


def _seq(x, meta_proj, g_mix, w_in_parts, params, w_out_parts, *, n_heads):
    b, seq, d = x.shape
    xbcm, zm, glum, dtm = meta_proj
    wz, wx, wg, wd = w_in_parts
    conv_dim, d_ssm, dc = wx.shape[1], wz.shape[1], wg.shape[1] // 2
    assert dc == SUBLANES * LANES
    nc = seq // CHUNK + 1
    front_pad = (-N_META) % CHUNK
    blk = lambda n: pl.BlockSpec((1, CHUNK, n), lambda i, c: (i, jnp.maximum(c - 1, 0), 0))
    nxt = pl.BlockSpec((1, CHUNK, d), lambda i, c: (i, jnp.minimum(c, nc - 2), 0))
    full = lambda a: pl.BlockSpec(a.shape, lambda i, c: (0,) * a.ndim)
    consts = [zm, xbcm, glum, dtm, g_mix, wz, wx, wg, wd, *params, *w_out_parts]
    proj_bufs = [pltpu.VMEM((CHUNK, w.shape[1]), F32) for w in (wz, wx, wg, wd)]
    kern = functools.partial(_seq_kernel, d_ssm=d_ssm, n_heads=n_heads, front_pad=front_pad)
    return pl.pallas_call(
        kern,
        grid=(b, nc),
        in_specs=[blk(d), nxt] + [full(a) for a in consts],
        out_specs=[blk(d),
                   pl.BlockSpec((1, n_heads, SSD_HEAD_DIM, SSD_STATE), lambda i, c: (i, 0, 0, 0)),
                   pl.BlockSpec((1, SSD_CONV - 1, conv_dim), lambda i, c: (i, 0, 0)),
                   pl.BlockSpec((1, CONF_WIDTH - 1, dc), lambda i, c: (i, 0, 0))],
        out_shape=[jax.ShapeDtypeStruct((b, seq, d), F32),
                   jax.ShapeDtypeStruct((b, n_heads, SSD_HEAD_DIM, SSD_STATE), F32),
                   jax.ShapeDtypeStruct((b, SSD_CONV - 1, conv_dim), F32),
                   jax.ShapeDtypeStruct((b, CONF_WIDTH - 1, dc), F32)],
        scratch_shapes=[pltpu.VMEM((SSD_STATE, d_ssm), F32),
                        pltpu.VMEM((SUBLANES + CHUNK, conv_dim), F32),
                        pltpu.VMEM(((32 + CHUNK) * SUBLANES, LANES), F32),
                        pltpu.VMEM((CHUNK * SUBLANES, LANES), F32)] + proj_bufs + proj_bufs,
        compiler_params=_cparams(("arbitrary", "arbitrary")),
        name="seq",
    )(x, x, *consts)


def _sample_kernel(xbc_ref, z_ref, glu_ref, dt_ref, ssm_ref, shist_ref, chist_ref,
                   cw_ref, cb_ref, dtb_ref, alog_ref, dskip_ref, gn_ref,
                   ccw_ref, ccb_ref, lng_ref, lnb_ref,
                   y_ref, c_ref, ssm_o, shist_o, chist_o, *, d_ssm, n_heads):
    sb = xbc_ref.shape[0]
    n = SSD_STATE
    gw = SSD_GROUPS * n
    hp = d_ssm // SSD_GROUPS
    heads_per_group = n_heads // SSD_GROUPS

    x_new = xbc_ref[...].astype(F32)
    acc = cb_ref[...] + cw_ref[SSD_CONV - 1:SSD_CONV, :] * x_new
    for k in range(SSD_CONV - 1):
        acc = acc + cw_ref[k:k + 1, :] * shist_ref[:, k, :]
    for k in range(SSD_CONV - 2):
        shist_o[:, k, :] = shist_ref[:, k + 1, :]
    shist_o[:, SSD_CONV - 2, :] = x_new
    xact = _silu(acc)
    xs = xact[:, :d_ssm]
    dtv = _softplus(dt_ref[...] + dtb_ref[...])
    a = -jnp.exp(alog_ref[...])
    decay = jnp.exp(dtv * a)

    hsel = (lax.broadcasted_iota(jnp.int32, (LANES, d_ssm), 1) // SSD_HEAD_DIM
            == lax.broadcasted_iota(jnp.int32, (LANES, d_ssm), 0)).astype(F32)
    dt_x = jnp.dot(dtv, hsel, preferred_element_type=F32, precision=HIGHEST)
    xdt = (xs * dt_x).astype(BF16)
    rowid = lax.broadcasted_iota(jnp.int32, (sb, 1), 0)

    y_rows = []
    for i in range(sb):
        sel = rowid == i
        xi = jnp.where(sel, xdt, jnp.zeros_like(xdt))
        parts = []
        for g in range(SSD_GROUPS):
            bm = xact[:, d_ssm + g * n:d_ssm + (g + 1) * n].astype(BF16)
            cmat = xact[:, d_ssm + gw + g * n:d_ssm + gw + (g + 1) * n].astype(BF16)
            outer = lax.dot_general(xi[:, g * hp:(g + 1) * hp], bm, (((0,), (0,)), ((), ())),
                                    preferred_element_type=F32)
            outer = outer.reshape(heads_per_group, SSD_HEAD_DIM, n)
            hs = slice(g * heads_per_group, (g + 1) * heads_per_group)
            s_old = ssm_ref[i, hs]
            dec_i = jnp.stack([jnp.broadcast_to(decay[i:i + 1, h:h + 1], (SSD_HEAD_DIM, n))
                               for h in range(g * heads_per_group, (g + 1) * heads_per_group)])
            s_new = s_old * dec_i + outer
            ssm_o[i, hs] = s_new
            yv = lax.dot_general(cmat, s_new.reshape(hp, n).astype(BF16), (((1,), (1,)), ((), ())),
                                 preferred_element_type=F32)
            parts.append(yv)
        yfull = jnp.concatenate(parts, axis=1)
        y_rows.append(jnp.where(sel, yfull, 0.0))
    y = y_rows[0]
    for r in y_rows[1:]:
        y = y + r
    y = y + dskip_ref[...] * xs
    y = y * _silu(z_ref[...].astype(F32))
    gsz = d_ssm // SSD_GROUPS
    y = jnp.concatenate(
        [_rms(y[:, g * gsz:(g + 1) * gsz], gn_ref[:, g * gsz:(g + 1) * gsz]) for g in range(SSD_GROUPS)],
        axis=1)
    y_ref[...] = y.astype(BF16)

    glu = glu_ref[...].astype(F32)
    dc = glu.shape[1] // 2
    cval = glu[:, :dc] * _sigmoid(glu[:, dc:])
    hist = CONF_WIDTH - 1
    cacc = ccb_ref[...] + ccw_ref[hist:hist + 1, :] * cval
    cacc = cacc + jnp.sum(chist_ref[...] * ccw_ref[0:hist, :][None], axis=1)
    chist_o[:, 0:hist - 1, :] = chist_ref[:, 1:hist, :]
    chist_o[:, hist - 1, :] = cval
    mu = jnp.mean(cacc, axis=-1, keepdims=True)
    var = jnp.mean(jnp.square(cacc - mu), axis=-1, keepdims=True)
    cn = (cacc - mu) * lax.rsqrt(var + EPS) * lng_ref[...] + lnb_ref[...]
    c_ref[...] = _silu(cn).astype(BF16)


def _sample(xbc, z, glu, dt, ssm, shist, chist, params, *, n_heads):
    bs, conv_dim = xbc.shape
    d_ssm = z.shape[1]
    dc = glu.shape[1] // 2
    sb = SUBLANES
    assert bs % sb == 0
    row = lambda n: pl.BlockSpec((sb, n), lambda i: (i, 0))
    full = lambda a: pl.BlockSpec(a.shape, lambda i: (0,) * a.ndim)
    st_spec = pl.BlockSpec((sb, n_heads, SSD_HEAD_DIM, SSD_STATE), lambda i: (i, 0, 0, 0))
    sh_spec = pl.BlockSpec((sb, SSD_CONV - 1, conv_dim), lambda i: (i, 0, 0))
    ch_spec = pl.BlockSpec((sb, CONF_WIDTH - 1, dc), lambda i: (i, 0, 0))
    kern = functools.partial(_sample_kernel, d_ssm=d_ssm, n_heads=n_heads)
    return pl.pallas_call(
        kern,
        grid=(bs // sb,),
        in_specs=[row(conv_dim), row(d_ssm), row(2 * dc), row(LANES), st_spec, sh_spec, ch_spec]
        + [full(p) for p in params],
        out_specs=[row(d_ssm), row(dc), st_spec, sh_spec, ch_spec],
        out_shape=[jax.ShapeDtypeStruct((bs, d_ssm), BF16),
                   jax.ShapeDtypeStruct((bs, dc), BF16),
                   jax.ShapeDtypeStruct(ssm.shape, F32),
                   jax.ShapeDtypeStruct(shist.shape, F32),
                   jax.ShapeDtypeStruct(chist.shape, F32)],
        compiler_params=_cparams(("parallel",)),
        name="sample_step",
    )(xbc, z, glu, dt, ssm, shist, chist, *params)


def _router_kernel(*refs, with_proj):
    if with_proj:
        y_ref, c_ref, res_ref, wy_ref, wc_ref = refs[:5]
        g_ref, wr_ref, br_ref, cnt_in_ref, _, h_ref = refs[5:11]
        xn_ref, idx_ref, gate_ref, rank_ref, cnt_ref, carry_ref = refs[11:]
        h = (res_ref[...]
             + jnp.dot(y_ref[...], wy_ref[...], preferred_element_type=F32)
             + jnp.dot(c_ref[...], wc_ref[...], preferred_element_type=F32))
        h_ref[...] = h
    else:
        h_in_ref, g_ref, wr_ref, br_ref, cnt_in_ref, _ = refs[:6]
        xn_ref, idx_ref, gate_ref, rank_ref, cnt_ref, carry_ref = refs[6:]
        h = h_in_ref[...]
    i = pl.program_id(0)

    @pl.when(i == 0)
    def _():
        carry_ref[...] = cnt_in_ref[...]

    xn = _rms(h, g_ref[...])
    _rows_to_tiles(xn_ref, xn)
    tm = h.shape[0]
    logits = lax.dot_general(wr_ref[...], xn, (((1,), (1,)), ((), ())),
                             preferred_element_type=F32, precision=HIGHEST) + br_ref[...]
    eid = lax.broadcasted_iota(jnp.int32, logits.shape, 0)
    vals, idxs, hots = [], [], []
    work = logits
    for _ in range(TOP_K):
        m = jnp.max(work, axis=0, keepdims=True)
        sel = jnp.min(jnp.where(work == m, eid, N_EXPERTS), axis=0, keepdims=True)
        hot = eid == sel
        work = jnp.where(hot, -jnp.inf, work)
        vals.append(m)
        idxs.append(sel)
        hots.append(hot)
    exps = [jnp.exp(v - vals[0]) for v in vals]
    den = exps[0]
    for e in exps[1:]:
        den = den + e
    chosen = hots[0]
    for hot in hots[1:]:
        chosen = jnp.logical_or(chosen, hot)
    chosen_f = chosen.astype(F32)
    r = lax.broadcasted_iota(jnp.int32, (tm, tm), 0)
    cidx = lax.broadcasted_iota(jnp.int32, (tm, tm), 1)
    upper = (r < cidx).astype(BF16)
    cum = jnp.dot(chosen_f.astype(BF16), upper, preferred_element_type=F32) + carry_ref[:, 0:1]
    for k in range(TOP_K):
        idx_ref[k:k + 1, :] = idxs[k]
        gate_ref[k:k + 1, :] = exps[k] / den
        rank_ref[k:k + 1, :] = jnp.sum(jnp.where(hots[k], cum, 0.0), axis=0, keepdims=True).astype(jnp.int32)
    carry_ref[...] = carry_ref[...] + jnp.sum(chosen_f, axis=1, keepdims=True)
    cnt_ref[...] = carry_ref[...]


def _router(h_or_proj, g, wr_t, br, cnt_in, xn_buf, row_off):
    with_proj = isinstance(h_or_proj, tuple)
    t, d = (h_or_proj[2] if with_proj else h_or_proj).shape
    assert d == SUBLANES * LANES
    tm = min(512, t)
    assert t % tm == 0 and row_off % tm == 0
    off = row_off // tm
    row = lambda n: pl.BlockSpec((tm, n), lambda i: (i, 0))
    col = pl.BlockSpec((TOP_K, tm), lambda i: (0, i))
    full = lambda a: pl.BlockSpec(a.shape, lambda i: (0,) * a.ndim)
    if with_proj:
        y, c, res, wy, wc = h_or_proj
        args = [y, c, res, wy, wc]
        in_specs = [row(y.shape[1]), row(c.shape[1]), row(d), full(wy), full(wc)]
        out_specs, out_shape = [row(d)], [jax.ShapeDtypeStruct((t, d), F32)]
    else:
        args, in_specs, out_specs, out_shape = [h_or_proj], [row(d)], [], []
    args += [g, wr_t, br, cnt_in, xn_buf]
    in_specs += [full(g), full(wr_t), full(br), full(cnt_in), pl.BlockSpec(memory_space=pl.ANY)]
    out_specs += [pl.BlockSpec((tm * SUBLANES, LANES), lambda i: (i + off, 0)), col, col, col, full(cnt_in)]
    out_shape += [jax.ShapeDtypeStruct(xn_buf.shape, F32),
                  jax.ShapeDtypeStruct((TOP_K, t), jnp.int32),
                  jax.ShapeDtypeStruct((TOP_K, t), F32),
                  jax.ShapeDtypeStruct((TOP_K, t), jnp.int32),
                  jax.ShapeDtypeStruct(cnt_in.shape, F32)]
    return pl.pallas_call(
        functools.partial(_router_kernel, with_proj=with_proj),
        grid=(t // tm,),
        in_specs=in_specs,
        out_specs=out_specs,
        out_shape=out_shape,
        scratch_shapes=[pltpu.VMEM(cnt_in.shape, F32)],
        input_output_aliases={len(args) - 1: len(out_shape) - 5},
        compiler_params=_cparams(("arbitrary",)),
        name="router",
    )(*args)


_INV_GROUP = 16


def _inverse_kernel(pad_lo_ref, pad_hi_ref, dest_ref, inv_ref, *, pad_base, fill_groups):
    step = pl.program_id(0)

    @pl.when(step == 0)
    def _():
        def sentinel(d, carry):
            inv_ref[d] = (pad_base + jnp.bitwise_and(d, MOE_BLOCK - 1)) * TOP_K
            return carry

        for e in range(pad_lo_ref.shape[0]):
            lax.fori_loop(pad_lo_ref[e], pad_hi_ref[e], sentinel, 0)

    @pl.when(step > 0)
    def _():
        def fill(jj, carry):
            a0 = ((step - 1) * fill_groups + jj) * _INV_GROUP
            for c in range(_INV_GROUP):
                inv_ref[dest_ref[a0 + c]] = a0 + c
            return carry

        lax.fori_loop(0, fill_groups, fill, 0)


def _build_inverse(dest, pad_lo, pad_hi, n_rows, pad_base):
    n_fill = dest.shape[0] // _INV_GROUP
    assert dest.shape[0] % _INV_GROUP == 0
    slices = max(g for g in range(1, 17) if n_fill % g == 0)
    kern = functools.partial(_inverse_kernel, pad_base=pad_base, fill_groups=n_fill // slices)
    smem = pl.BlockSpec(memory_space=pltpu.SMEM)
    return pl.pallas_call(
        kern,
        grid=(1 + slices,),
        in_specs=[smem, smem, smem],
        out_specs=smem,
        out_shape=jax.ShapeDtypeStruct((n_rows,), jnp.int32),
        compiler_params=_cparams(("arbitrary",)),
        name="inverse_table",
    )(pad_lo, pad_hi, dest)


def _moe_kernel(be_ref, na_ref, tok_ref, row_ref, eseq_ref, enext_ref,
                xn_hbm, wg_hbm, bg_ref, wu_hbm, bu_ref, wd_hbm, bd_ref,
                yk_hbm, wg_s, wu_s, wd_s, x0, x1, o0, o1, gsem, ssem, stg_g, stg_u, stg_d, wsem,
                *, trash_row):
    i = pl.program_id(0)
    n_active = na_ref[0]
    tab_rows = na_ref[1]
    last_block = pl.num_programs(0) - 2
    xbuf, obuf = (x0, x1), (o0, o1)
    prev = be_ref[jnp.maximum(i - 1, 0)]
    new_expert = jnp.logical_or(i == 0, be_ref[i] != prev)
    active = i < n_active

    def gather_start(block, slot):
        @pl.when(tab_rows > 0)
        def _():
            for r in range(MOE_BLOCK):
                src = pl.multiple_of(tok_ref[block * (MOE_BLOCK // LANES) + r // LANES, r % LANES], SUBLANES)
                pltpu.make_async_copy(xn_hbm.at[pl.ds(src, SUBLANES), :],
                                      xbuf[slot].at[pl.ds(r * SUBLANES, SUBLANES), :],
                                      gsem.at[slot]).start(priority=0)

    def gather_wait(slot):
        pltpu.make_async_copy(xn_hbm.at[pl.ds(0, MOE_BLOCK * SUBLANES), :], xbuf[slot],
                              gsem.at[slot]).wait()

    def scatter_start(block, slot, real, trash):
        for r in range(MOE_BLOCK):
            dst = jnp.where(real, row_ref[block * (MOE_BLOCK // LANES) + r // LANES, r % LANES],
                            (trash + r) * SUBLANES)
            pltpu.make_async_copy(obuf[slot].at[pl.ds(r * SUBLANES, SUBLANES), :],
                                  yk_hbm.at[pl.ds(pl.multiple_of(dst, SUBLANES), SUBLANES), :],
                                  ssem.at[slot]).start(priority=1)

    def scatter_wait(slot):
        pltpu.make_async_copy(obuf[slot], yk_hbm.at[pl.ds(0, MOE_BLOCK * SUBLANES), :],
                              ssem.at[slot]).wait()

    @pl.when(i == 0)
    def _():
        o0[...] = jnp.zeros_like(o0)
        o1[...] = jnp.zeros_like(o1)
        gather_start(0, 0)
        scatter_start(0, 0, False, trash_row)

    def weight_copies(expert, slot):
        return [pltpu.make_async_copy(w_hbm.at[expert], stg.at[slot], wsem.at[slot])
                for w_hbm, stg in ((wg_hbm, stg_g), (wu_hbm, stg_u), (wd_hbm, stg_d))]

    @pl.when(i == 0)
    def _():
        for cp in weight_copies(be_ref[0], 0):
            cp.start()

    @pl.when(jnp.logical_and(active, new_expert))
    def _():
        slot = eseq_ref[i] % 2
        for cp in weight_copies(be_ref[i], slot):
            cp.wait()
        wg_s[...] = stg_g[slot].astype(BF16)
        wu_s[...] = stg_u[slot].astype(BF16)
        wd_s[...] = stg_d[slot].astype(BF16)
        upcoming = enext_ref[i]

        @pl.when(upcoming >= 0)
        def _():
            for cp in weight_copies(upcoming, 1 - slot):
                cp.start(priority=1)

    for p in (0, 1):
        mine = (i % 2) == p

        @pl.when(jnp.logical_and(active, mine))
        def _():
            gather_wait(p)
            gather_start(jnp.minimum(i + 1, last_block), 1 - p)

        @pl.when(jnp.logical_and(active, mine))
        def _():
            scatter_start(jnp.maximum(i - 1, 0), 1 - p, i > 0, trash_row + MOE_BLOCK)
            x = _tiles_to_rows(xbuf[p], MOE_BLOCK).astype(BF16)
            gt = jnp.minimum(jnp.dot(x, wg_s[...], preferred_element_type=F32) + bg_ref[0], SWIGLU_LIMIT)
            up = jnp.clip(jnp.dot(x, wu_s[...], preferred_element_type=F32) + bu_ref[0],
                          -SWIGLU_LIMIT, SWIGLU_LIMIT)
            hdn = (up + 1.0) * gt * _sigmoid(gt * SWIGLU_ALPHA)
            y = jnp.dot(hdn.astype(BF16), wd_s[...], preferred_element_type=F32) + bd_ref[0]
            scatter_wait(p)
            _rows_to_tiles(obuf[p], y)

        @pl.when(jnp.logical_and(i == n_active, mine))
        def _():
            gather_wait(p)
            scatter_wait(p)
            scatter_start(i - 1, 1 - p, True, trash_row)
            scatter_wait(1 - p)


def _moe(block_expert, n_active, tok2d, row2d, expert_seq, expert_next, xn,
         w_gate, b_gate, w_up, b_up, w_down, b_down, *, yk_rows, trash_row):
    d = w_gate.shape[1]
    assert d == SUBLANES * LANES and xn.shape[1] == LANES
    nb = tok2d.shape[0] * LANES // MOE_BLOCK
    dff = w_gate.shape[2]
    bspec = lambda a: pl.BlockSpec((1,) + a.shape[1:], lambda i, be, *_: (be[i], 0, 0))
    hbm = pl.BlockSpec(memory_space=pl.ANY)
    blk = pltpu.VMEM((MOE_BLOCK * SUBLANES, LANES), F32)
    kern = functools.partial(_moe_kernel, trash_row=trash_row)
    return pl.pallas_call(
        kern,
        grid_spec=pltpu.PrefetchScalarGridSpec(
            num_scalar_prefetch=6,
            grid=(nb + 1,),
            in_specs=[hbm, hbm, bspec(b_gate), hbm, bspec(b_up), hbm, bspec(b_down)],
            out_specs=hbm,
            scratch_shapes=[pltpu.VMEM((d, dff), BF16), pltpu.VMEM((d, dff), BF16),
                            pltpu.VMEM((dff, d), BF16), blk, blk, blk, blk,
                            pltpu.SemaphoreType.DMA((2,)), pltpu.SemaphoreType.DMA((2,)),
                            pltpu.VMEM((2, d, dff), F32), pltpu.VMEM((2, d, dff), F32),
                            pltpu.VMEM((2, dff, d), F32), pltpu.SemaphoreType.DMA((2,))],
        ),
        out_shape=jax.ShapeDtypeStruct((yk_rows * SUBLANES, LANES), F32),
        compiler_params=_cparams(("arbitrary",)),
        name="moe_experts",
    )(block_expert, n_active, tok2d, row2d, expert_seq, expert_next, xn,
      w_gate, b_gate, w_up, b_up, w_down, b_down)


def _combine_kernel(h_ref, gate_ref, gf_ref, *rest):
    yk_refs, o_ref = rest[:TOP_K], rest[TOP_K]
    acc = h_ref[...]
    tm = acc.shape[0]
    for k in range(TOP_K):
        acc = acc + gate_ref[:, k:k + 1] * _tiles_to_rows(yk_refs[k], tm)
    o_ref[...] = _rms(acc, gf_ref[...])


def _combine(h, gates, g_final, yk, row_off, plane_tokens):
    t, d = h.shape
    tm = min(512, t)
    assert t % tm == 0
    plane = lambda k: pl.BlockSpec(
        (pl.Element(tm * SUBLANES), pl.Element(LANES)),
        lambda i: ((k * plane_tokens + row_off + i * tm) * SUBLANES, 0))
    return pl.pallas_call(
        _combine_kernel,
        grid=(t // tm,),
        in_specs=[pl.BlockSpec((tm, d), lambda i: (i, 0)),
                  pl.BlockSpec((tm, TOP_K), lambda i: (i, 0)),
                  pl.BlockSpec(g_final.shape, lambda i: (0, 0))] + [plane(k) for k in range(TOP_K)],
        out_specs=pl.BlockSpec((tm, d), lambda i: (i, 0)),
        out_shape=jax.ShapeDtypeStruct((t, d), F32),
        compiler_params=_cparams(("parallel",)),
        name="combine",
    )(h, gates, g_final, *([yk] * TOP_K))


def _pad_lanes(v, fill=0.0):
    v = v.reshape(1, -1).astype(F32)
    return jnp.pad(v, ((0, 0), (0, LANES - v.shape[1])), constant_values=fill)


def kernel(x_prompt, x_sample, state_ssm, state_ssd_conv, state_conf_conv, meta_tokens, g_mix, w_in,
           conv_ssd_w, conv_ssd_b, dt_bias, a_log, d_skip, g_ssd_norm, conv_conf_w, conv_conf_b,
           ln_conf_g, ln_conf_b, w_out, g_ffn, w_router, b_router, w_gate, b_gate, w_up, b_up,
           w_down, b_down, g_final):
    bp, seq, d = x_prompt.shape
    bs = x_sample.shape[0]
    depth = w_in.shape[0]
    assert depth == 1 and x_sample.shape[1] == 1 and seq % CHUNK == 0
    n_heads = a_log.shape[1]
    d_ssm = n_heads * SSD_HEAD_DIM
    conv_dim = conv_ssd_w.shape[2]
    dc = conv_conf_w.shape[2]
    l = 0

    o1, o2, o3 = d_ssm, d_ssm + conv_dim, d_ssm + conv_dim + n_heads
    w_in_l = w_in[l]
    wz = w_in_l[:, :o1].astype(BF16)
    wx = w_in_l[:, o1:o2].astype(BF16)
    wdt = jnp.pad(w_in_l[:, o2:o3], ((0, 0), (0, LANES - n_heads))).astype(BF16)
    wglu = w_in_l[:, o3:].astype(BF16)
    row2 = lambda v: v.reshape(1, -1).astype(F32)
    seq_params = (
        conv_ssd_w[l], row2(conv_ssd_b[l]), _pad_lanes(dt_bias[l]), _pad_lanes(a_log[l]),
        row2(jnp.repeat(d_skip[l], SSD_HEAD_DIM)), row2(g_ssd_norm[l]),
        jnp.pad(conv_conf_w[l], ((0, 32 - CONF_WIDTH), (0, 0))), row2(conv_conf_b[l]),
        row2(ln_conf_g[l]), row2(ln_conf_b[l]))
    g_mix_r = row2(g_mix[l])
    w_out_l = w_out[l]
    wy = w_out_l[:d_ssm].astype(BF16)
    wc = w_out_l[d_ssm:].astype(BF16)
    prompt_params = seq_params[:6] + (
        seq_params[6].reshape(32 * SUBLANES, LANES), conv_conf_b[l].reshape(SUBLANES, LANES).astype(F32)
    ) + seq_params[8:]

    front_pad = (-N_META) % CHUNK
    small = jnp.concatenate([jnp.zeros((front_pad, d), F32), meta_tokens.astype(F32),
                             x_sample.reshape(bs, d)], axis=0)
    z_m, xbc_m, glu_m, dt_m = _inproj(small, g_mix_r, wz, wx, wglu, wdt)

    h_p, ssm_p, shist_p, chist_p = _seq(
        x_prompt, (xbc_m[:CHUNK], z_m[:CHUNK], glu_m[:CHUNK], dt_m[:CHUNK]), g_mix_r,
        (wz, wx, wglu, wdt), prompt_params, (wy, wc), n_heads=n_heads)

    y_s, c_s, ssm_s, shist_s, chist_s = _sample(
        xbc_m[CHUNK:], z_m[CHUNK:], glu_m[CHUNK:], dt_m[CHUNK:],
        state_ssm[l], state_ssd_conv[l], state_conf_conv[l], seq_params, n_heads=n_heads)

    g_ffn_r = row2(g_ffn[l])
    wr_t = w_router[l].T.astype(F32)
    br = b_router[l].reshape(N_EXPERTS, 1).astype(F32)
    cnt0 = jnp.zeros((N_EXPERTS, LANES), F32)
    t_p = bp * seq
    t_all = t_p + bs
    assert (t_all * TOP_K) % LANES == 0
    h_p = h_p.reshape(t_p, d)
    xn_all = jnp.zeros((t_all * SUBLANES, LANES), F32)
    xn_all, idx_p, gate_p, rank_p, cnt1 = _router(h_p, g_ffn_r, wr_t, br, cnt0, xn_all, 0)
    h_s, xn_all, idx_s, gate_s, rank_s, cnt2 = _router(
        (y_s, c_s, x_sample.reshape(bs, d), wy, wc), g_ffn_r, wr_t, br, cnt1, xn_all, t_p)

    counts = cnt2[:, 0].astype(jnp.int32)
    padded = (counts + MOE_BLOCK - 1) // MOE_BLOCK * MOE_BLOCK
    pad_end = jnp.cumsum(padded)
    pad_start = pad_end - padded
    n_blocks = -(-(t_all * TOP_K) // MOE_BLOCK) + N_EXPERTS
    rows = n_blocks * MOE_BLOCK
    blk_first = jnp.arange(n_blocks + 1, dtype=jnp.int32) * MOE_BLOCK
    block_expert = jnp.minimum(jnp.sum(pad_end[None, :] <= blk_first[:, None], axis=1),
                               N_EXPERTS - 1).astype(jnp.int32)
    n_active = jnp.stack([pad_end[-1] // MOE_BLOCK,
                          jnp.asarray(MOE_BLOCK // LANES, jnp.int32)]).astype(jnp.int32)
    idx_all = jnp.concatenate([idx_p, idx_s], axis=1)
    rank_all = jnp.concatenate([rank_p, rank_s], axis=1)
    experts = jnp.arange(N_EXPERTS, dtype=jnp.int32)[:, None, None]
    start_of = jnp.sum(jnp.where(idx_all[None] == experts, pad_start[:, None, None], 0), axis=0)
    dest = (start_of + rank_all).T.reshape(-1).astype(jnp.int32)

    trash = TOP_K * t_all
    pad_lo = jnp.concatenate([pad_start + counts, pad_end[-1:]]).astype(jnp.int32)
    pad_hi = jnp.concatenate([pad_end, jnp.full((1,), rows, jnp.int32)]).astype(jnp.int32)
    inv = _build_inverse(dest, pad_lo, pad_hi, rows, t_all).reshape(rows // LANES, LANES)
    inv_tok = inv // TOP_K
    tok2d = jnp.minimum(inv_tok, t_all - 1) * SUBLANES
    row2d = jnp.where(inv_tok < t_all, (inv % TOP_K) * t_all + inv_tok, trash + inv_tok - t_all) * SUBLANES
    owns = counts > 0
    ids = jnp.arange(N_EXPERTS, dtype=jnp.int32)
    position = (jnp.cumsum(owns) - owns).astype(jnp.int32)
    later = jnp.where(owns[None, :] & (ids[None, :] > ids[:, None]), ids[None, :], N_EXPERTS)
    following = jnp.min(later, axis=1)
    following = jnp.where(following < N_EXPERTS, following, -1).astype(jnp.int32)
    hot = block_expert[:, None] == ids[None, :]
    expert_seq = jnp.sum(jnp.where(hot, position[None, :], 0), axis=1).astype(jnp.int32)
    expert_next = jnp.sum(jnp.where(hot, following[None, :], 0), axis=1).astype(jnp.int32)
    yk = _moe(block_expert, n_active, tok2d, row2d, expert_seq, expert_next, xn_all,
              w_gate[l], b_gate[l].reshape(N_EXPERTS, 1, -1), w_up[l], b_up[l].reshape(N_EXPERTS, 1, -1),
              w_down[l], b_down[l].reshape(N_EXPERTS, 1, -1), yk_rows=trash + 2 * MOE_BLOCK, trash_row=trash)
    g_fin = row2(g_final)
    y_prompt = _combine(h_p, gate_p.T, g_fin, yk, 0, t_all).reshape(bp, seq, d)
    y_sample = _combine(h_s, gate_s.T, g_fin, yk, t_p, t_all).reshape(bs, 1, d)

    return (y_prompt, y_sample, ssm_p[None], shist_p[None], chist_p[None],
            ssm_s[None], shist_s[None], chist_s[None])
```

```python
import functools

import jax
import jax.numpy as jnp
from jax import lax
from jax.experimental import pallas as pl
from jax.experimental.pallas import tpu as pltpu

F32 = jnp.float32
BF16 = jnp.bfloat16
HIGHEST = lax.Precision.HIGHEST

N_META = 16
SSD_HEAD_DIM = 64
SSD_GROUPS = 2
SSD_STATE = 128
SSD_CONV = 4
CHUNK = 128
CONF_WIDTH = 31
N_EXPERTS = 32
TOP_K = 4
SWIGLU_LIMIT = 7.0
SWIGLU_ALPHA = 1.702
EPS = 1e-5

LANES = 128
LANE_BITS = LANES.bit_length() - 1
MXU_COLS = 256
SUBLANES = 8
MOE_BLOCK = 256
VMEM_LIMIT = 56 * 1024 * 1024


def _cparams(sem):
    return pltpu.CompilerParams(dimension_semantics=sem, vmem_limit_bytes=VMEM_LIMIT)


def _sigmoid(x):
    return 1.0 / (1.0 + jnp.exp(-x))


def _tiles_to_rows(ref, n):
    return jnp.concatenate([ref[pl.ds(s, n, stride=SUBLANES), :] for s in range(SUBLANES)], axis=1)


def _rows_to_tiles(ref, val):
    n = val.shape[0]
    for s in range(SUBLANES):
        ref[pl.ds(s, n, stride=SUBLANES), :] = val[:, s * LANES:(s + 1) * LANES]


def _silu(x):
    return x * _sigmoid(x)


def _softplus(x):
    return jnp.maximum(x, 0.0) + jnp.log1p(jnp.exp(-jnp.abs(x)))


def _rms(x, g):
    return x * lax.rsqrt(jnp.mean(x * x, axis=-1, keepdims=True) + EPS) * g


def _inproj_kernel(x_ref, g_ref, wz_ref, wx_ref, wg_ref, wd_ref, z_ref, xbc_ref, glu_ref, dt_ref):
    u = _rms(x_ref[...], g_ref[...]).astype(BF16)
    z_ref[...] = jnp.dot(u, wz_ref[...], preferred_element_type=F32).astype(BF16)
    xbc_ref[...] = jnp.dot(u, wx_ref[...], preferred_element_type=F32).astype(BF16)
    glu_ref[...] = jnp.dot(u, wg_ref[...], preferred_element_type=F32).astype(BF16)
    dt_ref[...] = jnp.dot(u, wd_ref[...], preferred_element_type=F32)


def _inproj(x, g, wz, wx, wg, wd):
    t, d = x.shape
    tm = min(512, t)
    assert t % tm == 0
    row = lambda n: pl.BlockSpec((tm, n), lambda i: (i, 0))
    full = lambda a: pl.BlockSpec(a.shape, lambda i: (0, 0))
    return pl.pallas_call(
        _inproj_kernel,
        grid=(t // tm,),
        in_specs=[row(d), full(g), full(wz), full(wx), full(wg), full(wd)],
        out_specs=[row(wz.shape[1]), row(wx.shape[1]), row(wg.shape[1]), row(wd.shape[1])],
        out_shape=[jax.ShapeDtypeStruct((t, wz.shape[1]), BF16),
                   jax.ShapeDtypeStruct((t, wx.shape[1]), BF16),
                   jax.ShapeDtypeStruct((t, wg.shape[1]), BF16),
                   jax.ShapeDtypeStruct((t, wd.shape[1]), F32)],
        compiler_params=_cparams(("parallel",)),
        name="inproj",
    )(x, g, wz, wx, wg, wd)


def _seq_kernel(x_ref, xnext_ref, zm_ref, xbcm_ref, glum_ref, dtm_ref, gmix_ref,
                wz_ref, wx_ref, wg_ref, wd_ref, *rest, **static):
    c = pl.program_id(1)
    bufs = (rest[-8:-4], rest[-4:])
    meta = (zm_ref, xbcm_ref, glum_ref, dtm_ref)
    st_ref, xbuf_ref, cbuf_ref = rest[-12:-9]

    @pl.when(c == 0)
    def _():
        st_ref[...] = jnp.zeros_like(st_ref)
        xbuf_ref[...] = jnp.zeros_like(xbuf_ref)
        cbuf_ref[...] = jnp.zeros_like(cbuf_ref)

    for parity in (0, 1):

        @pl.when(c % 2 == parity)
        def _():
            cur, nxt = bufs[parity], bufs[1 - parity]
            proj = lambda s: jnp.where(c == 0, meta[s][...].astype(F32), cur[s][...])
            u_next = _rms(xnext_ref[0], gmix_ref[...]).astype(BF16)
            tiles = [(dst, w_ref, n0) for dst, w_ref in zip(nxt, (wz_ref, wx_ref, wg_ref, wd_ref))
                     for n0 in range(0, w_ref.shape[1], MXU_COLS)]
            pending = iter(tiles)

            def next_tile():
                item = next(pending, None)
                if item is not None:
                    dst, w_ref, n0 = item
                    n1 = min(n0 + MXU_COLS, w_ref.shape[1])
                    dst[:, n0:n1] = jnp.dot(u_next, w_ref[:, n0:n1], preferred_element_type=F32)

            _seq_chunk(proj, next_tile, x_ref, *rest[:-8], **static)
            for _ in tiles:
                next_tile()

    @pl.when(c == pl.num_programs(1) - 1)
    def _():
        ssm_ref, shist_ref, chist_ref = rest[-15:-12]
        hist = CONF_WIDTH - 1
        ssm_ref[0] = st_ref[...].T.reshape(static["n_heads"], SSD_HEAD_DIM, SSD_STATE)
        shist_ref[0] = xbuf_ref[SUBLANES - (SSD_CONV - 1):SUBLANES, :]
        chist_ref[0] = _tiles_to_rows(cbuf_ref.at[pl.ds((32 - hist) * SUBLANES, hist * SUBLANES), :], hist)


def _seq_chunk(proj, mxu_filler, x_ref,
               cw_ref, cb_ref, dtb_ref, alog_ref, dskip_ref, gn_ref,
               ccw_ref, ccb_ref, lng_ref, lnb_ref, wy_ref, wc_ref,
               h_ref, ssm_ref, shist_ref, chist_ref,
               st_ref, xbuf_ref, cbuf_ref, ctile_ref, *, d_ssm, n_heads, front_pad):
    c = pl.program_id(1)
    last = pl.num_programs(1) - 1
    is_meta = c == 0
    q = CHUNK
    n = SSD_STATE
    gw = SSD_GROUPS * n
    x_in = x_ref[0]

    xbc_raw = proj(1)
    xbuf_ref[SUBLANES:SUBLANES + q, :] = xbc_raw
    acc = cb_ref[...] + cw_ref[SSD_CONV - 1:SSD_CONV, :] * xbc_raw
    for j in range(1, SSD_CONV):
        mxu_filler()
        acc = acc + cw_ref[SSD_CONV - 1 - j:SSD_CONV - j, :] * xbuf_ref[SUBLANES - j:SUBLANES - j + q, :]
    xbuf_ref[0:SUBLANES, :] = xbuf_ref[q:q + SUBLANES, :]
    xact = _silu(acc)
    xs = xact[:, :d_ssm]
    bm = [xact[:, d_ssm + g * n:d_ssm + (g + 1) * n] for g in range(SSD_GROUPS)]
    cm = [xact[:, d_ssm + gw + g * n:d_ssm + gw + (g + 1) * n] for g in range(SSD_GROUPS)]

    dt_raw = proj(3)
    rows = lax.broadcasted_iota(jnp.int32, (q, LANES), 0)
    cols = lax.broadcasted_iota(jnp.int32, (q, LANES), 1)
    valid = jnp.logical_or(jnp.logical_not(is_meta), rows >= front_pad)
    dtv = jnp.where(valid, _softplus(dt_raw + dtb_ref[...]), 0.0)
    da = dtv * (-jnp.exp(alog_ref[...]))
    causal = rows >= cols
    tri = causal.astype(F32)
    a_cum = jnp.dot(tri, da, preferred_element_type=F32, precision=HIGHEST)
    a_cum_t = a_cum.T
    dt_t = dtv.T
    w_t = jnp.exp(a_cum_t[:, q - 1:q] - a_cum_t) * dt_t
    a_last = a_cum[q - 1:q, :]

    bm_t = [b.T for b in bm]
    cb = [lax.dot_general(cm[g].astype(BF16), bm[g].astype(BF16), (((1,), (1,)), ((), ())),
                          preferred_element_type=F32) for g in range(SSD_GROUPS)]
    lane = lax.broadcasted_iota(jnp.int32, (q, LANES), 1)
    lo = lane < SSD_HEAD_DIM
    heads_per_group = n_heads // SSD_GROUPS

    y_parts = []
    for j in range(n_heads // 2):
        g = (2 * j) // heads_per_group
        m_l, ce_l, bw_l = [], [], []
        for h in (2 * j, 2 * j + 1):
            col = a_cum[:, h:h + 1]
            seg = col - a_cum_t[h:h + 1, :]
            dec = jnp.exp(jnp.where(causal, seg, -jnp.inf))
            m_l.append((cb[g] * dec * dt_t[h:h + 1, :]).astype(BF16))
            ce_l.append((cm[g] * jnp.exp(col)).astype(BF16))
            bw_l.append((bm_t[g] * w_t[h:h + 1, :]).astype(BF16))
        sl = slice(j * LANES, (j + 1) * LANES)
        xs_pair = xs[:, sl]
        rhs_x = jnp.concatenate([jnp.where(lo, xs_pair, 0.0), jnp.where(lo, 0.0, xs_pair)],
                                axis=0).astype(BF16)
        st_pair = st_ref[:, sl]
        rhs_s = jnp.concatenate([jnp.where(lo, st_pair, 0.0), jnp.where(lo, 0.0, st_pair)],
                                axis=0).astype(BF16)
        lhs_y = jnp.concatenate(m_l + ce_l, axis=1)
        y_pair = jnp.dot(lhs_y, jnp.concatenate([rhs_x, rhs_s], axis=0), preferred_element_type=F32)
        st_new = jnp.dot(jnp.concatenate(bw_l, axis=1), rhs_x, preferred_element_type=F32)
        cd = jnp.exp(jnp.where(lo[0:1, :], a_last[:, 2 * j:2 * j + 1], a_last[:, 2 * j + 1:2 * j + 2]))
        st_ref[:, sl] = st_pair * cd + st_new
        y_parts.append(y_pair + dskip_ref[:, sl] * xs_pair)
        mxu_filler()

    y = jnp.concatenate(y_parts, axis=1)
    zf = proj(0)
    y = y * _silu(zf)
    gsz = d_ssm // SSD_GROUPS
    y = jnp.concatenate(
        [_rms(y[:, g * gsz:(g + 1) * gsz], gn_ref[:, g * gsz:(g + 1) * gsz]) for g in range(SSD_GROUPS)],
        axis=1)

    glu = proj(2)
    dc = glu.shape[1] // 2
    cval = glu[:, :dc] * _sigmoid(glu[:, dc:])
    hist = CONF_WIDTH - 1
    base = 32
    tile = lambda t: t * SUBLANES
    _rows_to_tiles(cbuf_ref.at[pl.ds(tile(base), tile(q)), :], cval)
    taps = ccw_ref[...].reshape(base, SUBLANES, LANES)
    tch = 16
    for t0 in range(0, q, tch):
        cacc = jnp.broadcast_to(ccb_ref[...][None], (tch, SUBLANES, LANES))
        for k in range(CONF_WIDTH):
            off = base - hist + k + t0
            window = cbuf_ref[pl.ds(tile(off), tile(tch)), :].reshape(tch, SUBLANES, LANES)
            cacc = cacc + taps[k][None] * window
        ctile_ref[pl.ds(tile(t0), tile(tch)), :] = cacc.reshape(tile(tch), LANES)
        mxu_filler()
    cbuf_ref[0:tile(base), :] = cbuf_ref[tile(q):tile(q + base), :]
    cacc = _tiles_to_rows(ctile_ref, q)
    mu = jnp.mean(cacc, axis=-1, keepdims=True)
    var = jnp.mean(jnp.square(cacc - mu), axis=-1, keepdims=True)
    cn = (cacc - mu) * lax.rsqrt(var + EPS) * lng_ref[...] + lnb_ref[...]
    cout = _silu(cn)

    h_ref[0] = (x_in + jnp.dot(y.astype(BF16), wy_ref[...], preferred_element_type=F32)
                + jnp.dot(cout.astype(BF16), wc_ref[...], preferred_element_type=F32))


def _seq(x, meta_proj, g_mix, w_in_parts, params, w_out_parts, *, n_heads):
    b, seq, d = x.shape
    xbcm, zm, glum, dtm = meta_proj
    wz, wx, wg, wd = w_in_parts
    conv_dim, d_ssm, dc = wx.shape[1], wz.shape[1], wg.shape[1] // 2
    assert dc == SUBLANES * LANES
    nc = seq // CHUNK + 1
    front_pad = (-N_META) % CHUNK
    blk = lambda n: pl.BlockSpec((1, CHUNK, n), lambda i, c: (i, jnp.maximum(c - 1, 0), 0))
    nxt = pl.BlockSpec((1, CHUNK, d), lambda i, c: (i, jnp.minimum(c, nc - 2), 0))
    full = lambda a: pl.BlockSpec(a.shape, lambda i, c: (0,) * a.ndim)
    consts = [zm, xbcm, glum, dtm, g_mix, wz, wx, wg, wd, *params, *w_out_parts]
    proj_bufs = [pltpu.VMEM((CHUNK, w.shape[1]), F32) for w in (wz, wx, wg, wd)]
    kern = functools.partial(_seq_kernel, d_ssm=d_ssm, n_heads=n_heads, front_pad=front_pad)
    return pl.pallas_call(
        kern,
        grid=(b, nc),
        in_specs=[blk(d), nxt] + [full(a) for a in consts],
        out_specs=[blk(d),
                   pl.BlockSpec((1, n_heads, SSD_HEAD_DIM, SSD_STATE), lambda i, c: (i, 0, 0, 0)),
                   pl.BlockSpec((1, SSD_CONV - 1, conv_dim), lambda i, c: (i, 0, 0)),
                   pl.BlockSpec((1, CONF_WIDTH - 1, dc), lambda i, c: (i, 0, 0))],
        out_shape=[jax.ShapeDtypeStruct((b, seq, d), F32),
                   jax.ShapeDtypeStruct((b, n_heads, SSD_HEAD_DIM, SSD_STATE), F32),
                   jax.ShapeDtypeStruct((b, SSD_CONV - 1, conv_dim), F32),
                   jax.ShapeDtypeStruct((b, CONF_WIDTH - 1, dc), F32)],
        scratch_shapes=[pltpu.VMEM((SSD_STATE, d_ssm), F32),
                        pltpu.VMEM((SUBLANES + CHUNK, conv_dim), F32),
                        pltpu.VMEM(((32 + CHUNK) * SUBLANES, LANES), F32),
                        pltpu.VMEM((CHUNK * SUBLANES, LANES), F32)] + proj_bufs + proj_bufs,
        compiler_params=_cparams(("arbitrary", "arbitrary")),
        name="seq",
    )(x, x, *consts)


def _sample_kernel(xbc_ref, z_ref, glu_ref, dt_ref, ssm_ref, shist_ref, chist_ref,
                   cw_ref, cb_ref, dtb_ref, alog_ref, dskip_ref, gn_ref,
                   ccw_ref, ccb_ref, lng_ref, lnb_ref,
                   y_ref, c_ref, ssm_o, shist_o, chist_o, *, d_ssm, n_heads):
    sb = xbc_ref.shape[0]
    n = SSD_STATE
    gw = SSD_GROUPS * n
    hp = d_ssm // SSD_GROUPS
    heads_per_group = n_heads // SSD_GROUPS

    x_new = xbc_ref[...].astype(F32)
    acc = cb_ref[...] + cw_ref[SSD_CONV - 1:SSD_CONV, :] * x_new
    for k in range(SSD_CONV - 1):
        acc = acc + cw_ref[k:k + 1, :] * shist_ref[:, k, :]
    for k in range(SSD_CONV - 2):
        shist_o[:, k, :] = shist_ref[:, k + 1, :]
    shist_o[:, SSD_CONV - 2, :] = x_new
    xact = _silu(acc)
    xs = xact[:, :d_ssm]
    dtv = _softplus(dt_ref[...] + dtb_ref[...])
    a = -jnp.exp(alog_ref[...])
    decay = jnp.exp(dtv * a)

    hsel = (lax.broadcasted_iota(jnp.int32, (LANES, d_ssm), 1) // SSD_HEAD_DIM
            == lax.broadcasted_iota(jnp.int32, (LANES, d_ssm), 0)).astype(F32)
    dt_x = jnp.dot(dtv, hsel, preferred_element_type=F32, precision=HIGHEST)
    xdt = (xs * dt_x).astype(BF16)
    rowid = lax.broadcasted_iota(jnp.int32, (sb, 1), 0)

    y_rows = []
    for i in range(sb):
        sel = rowid == i
        xi = jnp.where(sel, xdt, jnp.zeros_like(xdt))
        parts = []
        for g in range(SSD_GROUPS):
            bm = xact[:, d_ssm + g * n:d_ssm + (g + 1) * n].astype(BF16)
            cmat = xact[:, d_ssm + gw + g * n:d_ssm + gw + (g + 1) * n].astype(BF16)
            outer = lax.dot_general(xi[:, g * hp:(g + 1) * hp], bm, (((0,), (0,)), ((), ())),
                                    preferred_element_type=F32)
            outer = outer.reshape(heads_per_group, SSD_HEAD_DIM, n)
            hs = slice(g * heads_per_group, (g + 1) * heads_per_group)
            s_old = ssm_ref[i, hs]
            dec_i = jnp.stack([jnp.broadcast_to(decay[i:i + 1, h:h + 1], (SSD_HEAD_DIM, n))
                               for h in range(g * heads_per_group, (g + 1) * heads_per_group)])
            s_new = s_old * dec_i + outer
            ssm_o[i, hs] = s_new
            yv = lax.dot_general(cmat, s_new.reshape(hp, n).astype(BF16), (((1,), (1,)), ((), ())),
                                 preferred_element_type=F32)
            parts.append(yv)
        yfull = jnp.concatenate(parts, axis=1)
        y_rows.append(jnp.where(sel, yfull, 0.0))
    y = y_rows[0]
    for r in y_rows[1:]:
        y = y + r
    y = y + dskip_ref[...] * xs
    y = y * _silu(z_ref[...].astype(F32))
    gsz = d_ssm // SSD_GROUPS
    y = jnp.concatenate(
        [_rms(y[:, g * gsz:(g + 1) * gsz], gn_ref[:, g * gsz:(g + 1) * gsz]) for g in range(SSD_GROUPS)],
        axis=1)
    y_ref[...] = y.astype(BF16)

    glu = glu_ref[...].astype(F32)
    dc = glu.shape[1] // 2
    cval = glu[:, :dc] * _sigmoid(glu[:, dc:])
    hist = CONF_WIDTH - 1
    cacc = ccb_ref[...] + ccw_ref[hist:hist + 1, :] * cval
    cacc = cacc + jnp.sum(chist_ref[...] * ccw_ref[0:hist, :][None], axis=1)
    chist_o[:, 0:hist - 1, :] = chist_ref[:, 1:hist, :]
    chist_o[:, hist - 1, :] = cval
    mu = jnp.mean(cacc, axis=-1, keepdims=True)
    var = jnp.mean(jnp.square(cacc - mu), axis=-1, keepdims=True)
    cn = (cacc - mu) * lax.rsqrt(var + EPS) * lng_ref[...] + lnb_ref[...]
    c_ref[...] = _silu(cn).astype(BF16)


def _sample(xbc, z, glu, dt, ssm, shist, chist, params, *, n_heads):
    bs, conv_dim = xbc.shape
    d_ssm = z.shape[1]
    dc = glu.shape[1] // 2
    sb = SUBLANES
    assert bs % sb == 0
    row = lambda n: pl.BlockSpec((sb, n), lambda i: (i, 0))
    full = lambda a: pl.BlockSpec(a.shape, lambda i: (0,) * a.ndim)
    st_spec = pl.BlockSpec((sb, n_heads, SSD_HEAD_DIM, SSD_STATE), lambda i: (i, 0, 0, 0))
    sh_spec = pl.BlockSpec((sb, SSD_CONV - 1, conv_dim), lambda i: (i, 0, 0))
    ch_spec = pl.BlockSpec((sb, CONF_WIDTH - 1, dc), lambda i: (i, 0, 0))
    kern = functools.partial(_sample_kernel, d_ssm=d_ssm, n_heads=n_heads)
    return pl.pallas_call(
        kern,
        grid=(bs // sb,),
        in_specs=[row(conv_dim), row(d_ssm), row(2 * dc), row(LANES), st_spec, sh_spec, ch_spec]
        + [full(p) for p in params],
        out_specs=[row(d_ssm), row(dc), st_spec, sh_spec, ch_spec],
        out_shape=[jax.ShapeDtypeStruct((bs, d_ssm), BF16),
                   jax.ShapeDtypeStruct((bs, dc), BF16),
                   jax.ShapeDtypeStruct(ssm.shape, F32),
                   jax.ShapeDtypeStruct(shist.shape, F32),
                   jax.ShapeDtypeStruct(chist.shape, F32)],
        compiler_params=_cparams(("parallel",)),
        name="sample_step",
    )(xbc, z, glu, dt, ssm, shist, chist, *params)


def _router_kernel(*refs, with_proj):
    if with_proj:
        y_ref, c_ref, res_ref, wy_ref, wc_ref = refs[:5]
        g_ref, wr_ref, br_ref, cnt_in_ref, _, h_ref = refs[5:11]
        xn_ref, idx_ref, gate_ref, rank_ref, cnt_ref, carry_ref = refs[11:]
        h = (res_ref[...]
             + jnp.dot(y_ref[...], wy_ref[...], preferred_element_type=F32)
             + jnp.dot(c_ref[...], wc_ref[...], preferred_element_type=F32))
        h_ref[...] = h
    else:
        h_in_ref, g_ref, wr_ref, br_ref, cnt_in_ref, _ = refs[:6]
        xn_ref, idx_ref, gate_ref, rank_ref, cnt_ref, carry_ref = refs[6:]
        h = h_in_ref[...]
    i = pl.program_id(0)

    @pl.when(i == 0)
    def _():
        carry_ref[...] = cnt_in_ref[...]

    xn = _rms(h, g_ref[...])
    _rows_to_tiles(xn_ref, xn)
    tm = h.shape[0]
    logits = lax.dot_general(wr_ref[...], xn, (((1,), (1,)), ((), ())),
                             preferred_element_type=F32, precision=HIGHEST) + br_ref[...]
    eid = lax.broadcasted_iota(jnp.int32, logits.shape, 0)
    vals, idxs, hots = [], [], []
    work = logits
    for _ in range(TOP_K):
        m = jnp.max(work, axis=0, keepdims=True)
        sel = jnp.min(jnp.where(work == m, eid, N_EXPERTS), axis=0, keepdims=True)
        hot = eid == sel
        work = jnp.where(hot, -jnp.inf, work)
        vals.append(m)
        idxs.append(sel)
        hots.append(hot)
    exps = [jnp.exp(v - vals[0]) for v in vals]
    den = exps[0]
    for e in exps[1:]:
        den = den + e
    chosen = hots[0]
    for hot in hots[1:]:
        chosen = jnp.logical_or(chosen, hot)
    chosen_f = chosen.astype(F32)
    r = lax.broadcasted_iota(jnp.int32, (tm, tm), 0)
    cidx = lax.broadcasted_iota(jnp.int32, (tm, tm), 1)
    upper = (r < cidx).astype(BF16)
    cum = jnp.dot(chosen_f.astype(BF16), upper, preferred_element_type=F32) + carry_ref[:, 0:1]
    for k in range(TOP_K):
        idx_ref[k:k + 1, :] = idxs[k]
        gate_ref[k:k + 1, :] = exps[k] / den
        rank_ref[k:k + 1, :] = jnp.sum(jnp.where(hots[k], cum, 0.0), axis=0, keepdims=True).astype(jnp.int32)
    carry_ref[...] = carry_ref[...] + jnp.sum(chosen_f, axis=1, keepdims=True)
    cnt_ref[...] = carry_ref[...]


def _router(h_or_proj, g, wr_t, br, cnt_in, xn_buf, row_off):
    with_proj = isinstance(h_or_proj, tuple)
    t, d = (h_or_proj[2] if with_proj else h_or_proj).shape
    assert d == SUBLANES * LANES
    tm = min(512, t)
    assert t % tm == 0 and row_off % tm == 0
    off = row_off // tm
    row = lambda n: pl.BlockSpec((tm, n), lambda i: (i, 0))
    col = pl.BlockSpec((TOP_K, tm), lambda i: (0, i))
    full = lambda a: pl.BlockSpec(a.shape, lambda i: (0,) * a.ndim)
    if with_proj:
        y, c, res, wy, wc = h_or_proj
        args = [y, c, res, wy, wc]
        in_specs = [row(y.shape[1]), row(c.shape[1]), row(d), full(wy), full(wc)]
        out_specs, out_shape = [row(d)], [jax.ShapeDtypeStruct((t, d), F32)]
    else:
        args, in_specs, out_specs, out_shape = [h_or_proj], [row(d)], [], []
    args += [g, wr_t, br, cnt_in, xn_buf]
    in_specs += [full(g), full(wr_t), full(br), full(cnt_in), pl.BlockSpec(memory_space=pl.ANY)]
    out_specs += [pl.BlockSpec((tm * SUBLANES, LANES), lambda i: (i + off, 0)), col, col, col, full(cnt_in)]
    out_shape += [jax.ShapeDtypeStruct(xn_buf.shape, F32),
                  jax.ShapeDtypeStruct((TOP_K, t), jnp.int32),
                  jax.ShapeDtypeStruct((TOP_K, t), F32),
                  jax.ShapeDtypeStruct((TOP_K, t), jnp.int32),
                  jax.ShapeDtypeStruct(cnt_in.shape, F32)]
    return pl.pallas_call(
        functools.partial(_router_kernel, with_proj=with_proj),
        grid=(t // tm,),
        in_specs=in_specs,
        out_specs=out_specs,
        out_shape=out_shape,
        scratch_shapes=[pltpu.VMEM(cnt_in.shape, F32)],
        input_output_aliases={len(args) - 1: len(out_shape) - 5},
        compiler_params=_cparams(("arbitrary",)),
        name="router",
    )(*args)


_INV_GROUP = 16


def _inverse_kernel(pad_lo_ref, pad_hi_ref, dest_ref, inv_ref, *, pad_base, fill_groups):
    step = pl.program_id(0)

    @pl.when(step == 0)
    def _():
        def sentinel(d, carry):
            inv_ref[d] = (pad_base + jnp.bitwise_and(d, MOE_BLOCK - 1)) * TOP_K
            return carry

        for e in range(pad_lo_ref.shape[0]):
            lax.fori_loop(pad_lo_ref[e], pad_hi_ref[e], sentinel, 0)

    @pl.when(step > 0)
    def _():
        def fill(jj, carry):
            a0 = ((step - 1) * fill_groups + jj) * _INV_GROUP
            for c in range(_INV_GROUP):
                inv_ref[dest_ref[a0 + c]] = a0 + c
            return carry

        lax.fori_loop(0, fill_groups, fill, 0)


def _build_inverse(dest, pad_lo, pad_hi, n_rows, pad_base):
    n_fill = dest.shape[0] // _INV_GROUP
    assert dest.shape[0] % _INV_GROUP == 0
    slices = max(g for g in range(1, 17) if n_fill % g == 0)
    kern = functools.partial(_inverse_kernel, pad_base=pad_base, fill_groups=n_fill // slices)
    smem = pl.BlockSpec(memory_space=pltpu.SMEM)
    return pl.pallas_call(
        kern,
        grid=(1 + slices,),
        in_specs=[smem, smem, smem],
        out_specs=smem,
        out_shape=jax.ShapeDtypeStruct((n_rows,), jnp.int32),
        compiler_params=_cparams(("arbitrary",)),
        name="inverse_table",
    )(pad_lo, pad_hi, dest)


def _moe_kernel(be_ref, na_ref, tok_ref, row_ref, eseq_ref, enext_ref,
                xn_hbm, wg_hbm, bg_ref, wu_hbm, bu_ref, wd_hbm, bd_ref,
                yk_hbm, wg_s, wu_s, wd_s, x0, x1, x2, x3, o0, o1, gsem, ssem, stg_g, stg_u, stg_d, wsem,
                *, trash_row):
    i = pl.program_id(0)
    n_active = na_ref[0]
    last_block = pl.num_programs(0) - 2
    xbuf, obuf = (x0, x1, x2, x3), (o0, o1)
    depth = len(xbuf)
    prev = be_ref[jnp.maximum(i - 1, 0)]
    new_expert = jnp.logical_or(i == 0, be_ref[i] != prev)
    active = i < n_active

    def gather_start(block, slot):
        for r in range(MOE_BLOCK):
            src = pl.multiple_of(tok_ref[block * (MOE_BLOCK // LANES) + r // LANES, r % LANES], SUBLANES)
            pltpu.make_async_copy(xn_hbm.at[pl.ds(src, SUBLANES), :],
                                  xbuf[slot].at[pl.ds(r * SUBLANES, SUBLANES), :],
                                  gsem.at[slot]).start(priority=0)

    def gather_wait(slot):
        pltpu.make_async_copy(xn_hbm.at[pl.ds(0, MOE_BLOCK * SUBLANES), :], xbuf[slot],
                              gsem.at[slot]).wait()

    def scatter_start(block, slot, real, trash):
        for r in range(MOE_BLOCK):
            dst = jnp.where(real, row_ref[block * (MOE_BLOCK // LANES) + r // LANES, r % LANES],
                            (trash + r) * SUBLANES)
            pltpu.make_async_copy(obuf[slot].at[pl.ds(r * SUBLANES, SUBLANES), :],
                                  yk_hbm.at[pl.ds(pl.multiple_of(dst, SUBLANES), SUBLANES), :],
                                  ssem.at[slot]).start(priority=1)

    def scatter_wait(slot):
        pltpu.make_async_copy(obuf[slot], yk_hbm.at[pl.ds(0, MOE_BLOCK * SUBLANES), :],
                              ssem.at[slot]).wait()

    @pl.when(i == 0)
    def _():
        o0[...] = jnp.zeros_like(o0)
        o1[...] = jnp.zeros_like(o1)
        gather_start(0, 0)
        gather_start(jnp.minimum(1, last_block), 1)
        scatter_start(0, 0, False, trash_row)

    def weight_copies(expert, slot):
        return [pltpu.make_async_copy(w_hbm.at[expert], stg.at[slot], wsem.at[slot])
                for w_hbm, stg in ((wg_hbm, stg_g), (wu_hbm, stg_u), (wd_hbm, stg_d))]

    @pl.when(i == 0)
    def _():
        for cp in weight_copies(be_ref[0], 0):
            cp.start()

    @pl.when(jnp.logical_and(active, new_expert))
    def _():
        slot = eseq_ref[i] % 2
        for cp in weight_copies(be_ref[i], slot):
            cp.wait()
        wg_s[...] = stg_g[slot].astype(BF16)
        wu_s[...] = stg_u[slot].astype(BF16)
        wd_s[...] = stg_d[slot].astype(BF16)
        upcoming = enext_ref[i]

        @pl.when(upcoming >= 0)
        def _():
            for cp in weight_copies(upcoming, 1 - slot):
                cp.start(priority=1)

    for q in range(depth):
        mine = (i % depth) == q
        p = q % 2

        @pl.when(jnp.logical_and(active, mine))
        def _():
            gather_wait(q)
            gather_start(jnp.minimum(i + 2, last_block), (q + 2) % depth)
            x = _tiles_to_rows(xbuf[q], MOE_BLOCK).astype(BF16)
            gt = jnp.minimum(jnp.dot(x, wg_s[...], preferred_element_type=F32) + bg_ref[0], SWIGLU_LIMIT)
            up = jnp.clip(jnp.dot(x, wu_s[...], preferred_element_type=F32) + bu_ref[0],
                          -SWIGLU_LIMIT, SWIGLU_LIMIT)
            hdn = (up + 1.0) * gt * _sigmoid(gt * SWIGLU_ALPHA)
            scatter_start(jnp.maximum(i - 1, 0), 1 - p, i > 0, trash_row + MOE_BLOCK)
            y = jnp.dot(hdn.astype(BF16), wd_s[...], preferred_element_type=F32) + bd_ref[0]
            scatter_wait(p)
            _rows_to_tiles(obuf[p], y)

        @pl.when(jnp.logical_and(i == n_active, mine))
        def _():
            gather_wait(q)
            gather_wait((q + 1) % depth)
            scatter_wait(p)
            scatter_start(i - 1, 1 - p, True, trash_row)
            scatter_wait(1 - p)


def _moe(block_expert, n_active, tok2d, row2d, expert_seq, expert_next, xn,
         w_gate, b_gate, w_up, b_up, w_down, b_down, *, yk_rows, trash_row):
    d = w_gate.shape[1]
    assert d == SUBLANES * LANES and xn.shape[1] == LANES
    nb = tok2d.shape[0] * LANES // MOE_BLOCK
    dff = w_gate.shape[2]
    bspec = lambda a: pl.BlockSpec((1,) + a.shape[1:], lambda i, be, *_: (be[i], 0, 0))
    hbm = pl.BlockSpec(memory_space=pl.ANY)
    blk = pltpu.VMEM((MOE_BLOCK * SUBLANES, LANES), F32)
    kern = functools.partial(_moe_kernel, trash_row=trash_row)
    return pl.pallas_call(
        kern,
        grid_spec=pltpu.PrefetchScalarGridSpec(
            num_scalar_prefetch=6,
            grid=(nb + 1,),
            in_specs=[hbm, hbm, bspec(b_gate), hbm, bspec(b_up), hbm, bspec(b_down)],
            out_specs=hbm,
            scratch_shapes=[pltpu.VMEM((d, dff), BF16), pltpu.VMEM((d, dff), BF16),
                            pltpu.VMEM((dff, d), BF16), blk, blk, blk, blk, blk, blk,
                            pltpu.SemaphoreType.DMA((4,)), pltpu.SemaphoreType.DMA((2,)),
                            pltpu.VMEM((2, d, dff), F32), pltpu.VMEM((2, d, dff), F32),
                            pltpu.VMEM((2, dff, d), F32), pltpu.SemaphoreType.DMA((2,))],
        ),
        out_shape=jax.ShapeDtypeStruct((yk_rows * SUBLANES, LANES), F32),
        compiler_params=_cparams(("arbitrary",)),
        name="moe_experts",
    )(block_expert, n_active, tok2d, row2d, expert_seq, expert_next, xn,
      w_gate, b_gate, w_up, b_up, w_down, b_down)


def _combine_kernel(h_ref, gate_ref, gf_ref, *rest):
    yk_refs, o_ref = rest[:TOP_K], rest[TOP_K]
    acc = h_ref[...]
    tm = acc.shape[0]
    for k in range(TOP_K):
        acc = acc + gate_ref[:, k:k + 1] * _tiles_to_rows(yk_refs[k], tm)
    o_ref[...] = _rms(acc, gf_ref[...])


def _combine(h, gates, g_final, yk, row_off, plane_tokens):
    t, d = h.shape
    tm = min(512, t)
    assert t % tm == 0
    plane = lambda k: pl.BlockSpec(
        (pl.Element(tm * SUBLANES), pl.Element(LANES)),
        lambda i: ((k * plane_tokens + row_off + i * tm) * SUBLANES, 0))
    return pl.pallas_call(
        _combine_kernel,
        grid=(t // tm,),
        in_specs=[pl.BlockSpec((tm, d), lambda i: (i, 0)),
                  pl.BlockSpec((tm, TOP_K), lambda i: (i, 0)),
                  pl.BlockSpec(g_final.shape, lambda i: (0, 0))] + [plane(k) for k in range(TOP_K)],
        out_specs=pl.BlockSpec((tm, d), lambda i: (i, 0)),
        out_shape=jax.ShapeDtypeStruct((t, d), F32),
        compiler_params=_cparams(("parallel",)),
        name="combine",
    )(h, gates, g_final, *([yk] * TOP_K))


def _pad_lanes(v, fill=0.0):
    v = v.reshape(1, -1).astype(F32)
    return jnp.pad(v, ((0, 0), (0, LANES - v.shape[1])), constant_values=fill)


def kernel(x_prompt, x_sample, state_ssm, state_ssd_conv, state_conf_conv, meta_tokens, g_mix, w_in,
           conv_ssd_w, conv_ssd_b, dt_bias, a_log, d_skip, g_ssd_norm, conv_conf_w, conv_conf_b,
           ln_conf_g, ln_conf_b, w_out, g_ffn, w_router, b_router, w_gate, b_gate, w_up, b_up,
           w_down, b_down, g_final):
    bp, seq, d = x_prompt.shape
    bs = x_sample.shape[0]
    depth = w_in.shape[0]
    assert depth == 1 and x_sample.shape[1] == 1 and seq % CHUNK == 0
    n_heads = a_log.shape[1]
    d_ssm = n_heads * SSD_HEAD_DIM
    conv_dim = conv_ssd_w.shape[2]
    dc = conv_conf_w.shape[2]
    l = 0

    o1, o2, o3 = d_ssm, d_ssm + conv_dim, d_ssm + conv_dim + n_heads
    w_in_l = w_in[l]
    wz = w_in_l[:, :o1].astype(BF16)
    wx = w_in_l[:, o1:o2].astype(BF16)
    wdt = jnp.pad(w_in_l[:, o2:o3], ((0, 0), (0, LANES - n_heads))).astype(BF16)
    wglu = w_in_l[:, o3:].astype(BF16)
    row2 = lambda v: v.reshape(1, -1).astype(F32)
    seq_params = (
        conv_ssd_w[l], row2(conv_ssd_b[l]), _pad_lanes(dt_bias[l]), _pad_lanes(a_log[l]),
        row2(jnp.repeat(d_skip[l], SSD_HEAD_DIM)), row2(g_ssd_norm[l]),
        jnp.pad(conv_conf_w[l], ((0, 32 - CONF_WIDTH), (0, 0))), row2(conv_conf_b[l]),
        row2(ln_conf_g[l]), row2(ln_conf_b[l]))
    g_mix_r = row2(g_mix[l])
    w_out_l = w_out[l]
    wy = w_out_l[:d_ssm].astype(BF16)
    wc = w_out_l[d_ssm:].astype(BF16)
    prompt_params = seq_params[:6] + (
        seq_params[6].reshape(32 * SUBLANES, LANES), conv_conf_b[l].reshape(SUBLANES, LANES).astype(F32)
    ) + seq_params[8:]

    front_pad = (-N_META) % CHUNK
    small = jnp.concatenate([jnp.zeros((front_pad, d), F32), meta_tokens.astype(F32),
                             x_sample.reshape(bs, d)], axis=0)
    z_m, xbc_m, glu_m, dt_m = _inproj(small, g_mix_r, wz, wx, wglu, wdt)

    h_p, ssm_p, shist_p, chist_p = _seq(
        x_prompt, (xbc_m[:CHUNK], z_m[:CHUNK], glu_m[:CHUNK], dt_m[:CHUNK]), g_mix_r,
        (wz, wx, wglu, wdt), prompt_params, (wy, wc), n_heads=n_heads)

    y_s, c_s, ssm_s, shist_s, chist_s = _sample(
        xbc_m[CHUNK:], z_m[CHUNK:], glu_m[CHUNK:], dt_m[CHUNK:],
        state_ssm[l], state_ssd_conv[l], state_conf_conv[l], seq_params, n_heads=n_heads)

    g_ffn_r = row2(g_ffn[l])
    wr_t = w_router[l].T.astype(F32)
    br = b_router[l].reshape(N_EXPERTS, 1).astype(F32)
    cnt0 = jnp.zeros((N_EXPERTS, LANES), F32)
    t_p = bp * seq
    t_all = t_p + bs
    assert (t_all * TOP_K) % LANES == 0
    h_p = h_p.reshape(t_p, d)
    xn_all = jnp.zeros((t_all * SUBLANES, LANES), F32)
    xn_all, idx_p, gate_p, rank_p, cnt1 = _router(h_p, g_ffn_r, wr_t, br, cnt0, xn_all, 0)
    h_s, xn_all, idx_s, gate_s, rank_s, cnt2 = _router(
        (y_s, c_s, x_sample.reshape(bs, d), wy, wc), g_ffn_r, wr_t, br, cnt1, xn_all, t_p)

    counts = cnt2[:, 0].astype(jnp.int32)
    padded = (counts + MOE_BLOCK - 1) // MOE_BLOCK * MOE_BLOCK
    pad_end = jnp.cumsum(padded)
    pad_start = pad_end - padded
    n_blocks = -(-(t_all * TOP_K) // MOE_BLOCK) + N_EXPERTS
    rows = n_blocks * MOE_BLOCK
    blk_first = jnp.arange(n_blocks + 1, dtype=jnp.int32) * MOE_BLOCK
    block_expert = jnp.minimum(jnp.sum(pad_end[None, :] <= blk_first[:, None], axis=1),
                               N_EXPERTS - 1).astype(jnp.int32)
    n_active = (pad_end[-1] // MOE_BLOCK).astype(jnp.int32).reshape(1)
    idx_all = jnp.concatenate([idx_p, idx_s], axis=1)
    rank_all = jnp.concatenate([rank_p, rank_s], axis=1)
    experts = jnp.arange(N_EXPERTS, dtype=jnp.int32)[:, None, None]
    start_of = jnp.sum(jnp.where(idx_all[None] == experts, pad_start[:, None, None], 0), axis=0)
    dest = (start_of + rank_all).T.reshape(-1).astype(jnp.int32)

    trash = TOP_K * t_all
    pad_lo = jnp.concatenate([pad_start + counts, pad_end[-1:]]).astype(jnp.int32)
    pad_hi = jnp.concatenate([pad_end, jnp.full((1,), rows, jnp.int32)]).astype(jnp.int32)
    inv = _build_inverse(dest, pad_lo, pad_hi, rows, t_all).reshape(rows // LANES, LANES)
    inv_tok = inv // TOP_K
    tok2d = jnp.minimum(inv_tok, t_all - 1) * SUBLANES
    row2d = jnp.where(inv_tok < t_all, (inv % TOP_K) * t_all + inv_tok, trash + inv_tok - t_all) * SUBLANES
    owns = counts > 0
    ids = jnp.arange(N_EXPERTS, dtype=jnp.int32)
    position = (jnp.cumsum(owns) - owns).astype(jnp.int32)
    later = jnp.where(owns[None, :] & (ids[None, :] > ids[:, None]), ids[None, :], N_EXPERTS)
    following = jnp.min(later, axis=1)
    following = jnp.where(following < N_EXPERTS, following, -1).astype(jnp.int32)
    hot = block_expert[:, None] == ids[None, :]
    expert_seq = jnp.sum(jnp.where(hot, position[None, :], 0), axis=1).astype(jnp.int32)
    expert_next = jnp.sum(jnp.where(hot, following[None, :], 0), axis=1).astype(jnp.int32)
    yk = _moe(block_expert, n_active, tok2d, row2d, expert_seq, expert_next, xn_all,
              w_gate[l], b_gate[l].reshape(N_EXPERTS, 1, -1), w_up[l], b_up[l].reshape(N_EXPERTS, 1, -1),
              w_down[l], b_down[l].reshape(N_EXPERTS, 1, -1), yk_rows=trash + 2 * MOE_BLOCK, trash_row=trash)
    g_fin = row2(g_final)
    y_prompt = _combine(h_p, gate_p.T, g_fin, yk, 0, t_all).reshape(bp, seq, d)
    y_sample = _combine(h_s, gate_s.T, g_fin, yk, t_p, t_all).reshape(bs, 1, d)

    return (y_prompt, y_sample, ssm_p[None], shist_p[None], chist_p[None],
            ssm_s[None], shist_s[None], chist_s[None])
```

```python
import functools

import jax
import jax.numpy as jnp
from jax import lax
from jax.experimental import pallas as pl
from jax.experimental.pallas import tpu as pltpu

F32 = jnp.float32
BF16 = jnp.bfloat16
HIGHEST = lax.Precision.HIGHEST

N_META = 16
SSD_HEAD_DIM = 64
SSD_GROUPS = 2
SSD_STATE = 128
SSD_CONV = 4
CHUNK = 128
CONF_WIDTH = 31
N_EXPERTS = 32
TOP_K = 4
SWIGLU_LIMIT = 7.0
SWIGLU_ALPHA = 1.702
EPS = 1e-5

LANES = 128
LANE_BITS = LANES.bit_length() - 1
MXU_COLS = 256
SUBLANES = 8
MOE_BLOCK = 256
VMEM_LIMIT = 56 * 1024 * 1024


def _cparams(sem):
    return pltpu.CompilerParams(dimension_semantics=sem, vmem_limit_bytes=VMEM_LIMIT)


def _sigmoid(x):
    return 1.0 / (1.0 + jnp.exp(-x))


def _tiles_to_rows(ref, n):
    return jnp.concatenate([ref[pl.ds(s, n, stride=SUBLANES), :] for s in range(SUBLANES)], axis=1)


def _rows_to_tiles(ref, val):
    n = val.shape[0]
    for s in range(SUBLANES):
        ref[pl.ds(s, n, stride=SUBLANES), :] = val[:, s * LANES:(s + 1) * LANES]


def _silu(x):
    return x * _sigmoid(x)


def _softplus(x):
    return jnp.maximum(x, 0.0) + jnp.log1p(jnp.exp(-jnp.abs(x)))


def _rms(x, g):
    return x * lax.rsqrt(jnp.mean(x * x, axis=-1, keepdims=True) + EPS) * g


def _inproj_kernel(x_ref, g_ref, wz_ref, wx_ref, wg_ref, wd_ref, z_ref, xbc_ref, glu_ref, dt_ref):
    u = _rms(x_ref[...], g_ref[...]).astype(BF16)
    z_ref[...] = jnp.dot(u, wz_ref[...], preferred_element_type=F32).astype(BF16)
    xbc_ref[...] = jnp.dot(u, wx_ref[...], preferred_element_type=F32).astype(BF16)
    glu_ref[...] = jnp.dot(u, wg_ref[...], preferred_element_type=F32).astype(BF16)
    dt_ref[...] = jnp.dot(u, wd_ref[...], preferred_element_type=F32)


def _inproj(x, g, wz, wx, wg, wd):
    t, d = x.shape
    tm = min(512, t)
    assert t % tm == 0
    row = lambda n: pl.BlockSpec((tm, n), lambda i: (i, 0))
    full = lambda a: pl.BlockSpec(a.shape, lambda i: (0, 0))
    return pl.pallas_call(
        _inproj_kernel,
        grid=(t // tm,),
        in_specs=[row(d), full(g), full(wz), full(wx), full(wg), full(wd)],
        out_specs=[row(wz.shape[1]), row(wx.shape[1]), row(wg.shape[1]), row(wd.shape[1])],
        out_shape=[jax.ShapeDtypeStruct((t, wz.shape[1]), BF16),
                   jax.ShapeDtypeStruct((t, wx.shape[1]), BF16),
                   jax.ShapeDtypeStruct((t, wg.shape[1]), BF16),
                   jax.ShapeDtypeStruct((t, wd.shape[1]), F32)],
        compiler_params=_cparams(("parallel",)),
        name="inproj",
    )(x, g, wz, wx, wg, wd)


def _seq_kernel(x_ref, xnext_ref, zm_ref, xbcm_ref, glum_ref, dtm_ref, gmix_ref,
                wz_ref, wx_ref, wg_ref, wd_ref, *rest, **static):
    c = pl.program_id(1)
    bufs = (rest[-8:-4], rest[-4:])
    meta = (zm_ref, xbcm_ref, glum_ref, dtm_ref)
    st_ref, xbuf_ref, cbuf_ref = rest[-12:-9]

    @pl.when(c == 0)
    def _():
        st_ref[...] = jnp.zeros_like(st_ref)
        xbuf_ref[...] = jnp.zeros_like(xbuf_ref)
        cbuf_ref[...] = jnp.zeros_like(cbuf_ref)

    for parity in (0, 1):

        @pl.when(c % 2 == parity)
        def _():
            cur, nxt = bufs[parity], bufs[1 - parity]
            proj = lambda s: jnp.where(c == 0, meta[s][...].astype(F32), cur[s][...])
            u_next = _rms(xnext_ref[0], gmix_ref[...]).astype(BF16)
            tiles = [(dst, w_ref, n0) for dst, w_ref in zip(nxt, (wz_ref, wx_ref, wg_ref, wd_ref))
                     for n0 in range(0, w_ref.shape[1], MXU_COLS)]
            pending = iter(tiles)

            def next_tile():
                item = next(pending, None)
                if item is not None:
                    dst, w_ref, n0 = item
                    n1 = min(n0 + MXU_COLS, w_ref.shape[1])
                    dst[:, n0:n1] = jnp.dot(u_next, w_ref[:, n0:n1], preferred_element_type=F32)

            _seq_chunk(proj, next_tile, x_ref, *rest[:-8], **static)
            for _ in tiles:
                next_tile()

    @pl.when(c == pl.num_programs(1) - 1)
    def _():
        ssm_ref, shist_ref, chist_ref = rest[-15:-12]
        hist = CONF_WIDTH - 1
        ssm_ref[0] = st_ref[...].T.reshape(static["n_heads"], SSD_HEAD_DIM, SSD_STATE)
        shist_ref[0] = xbuf_ref[SUBLANES - (SSD_CONV - 1):SUBLANES, :]
        chist_ref[0] = _tiles_to_rows(cbuf_ref.at[pl.ds((32 - hist) * SUBLANES, hist * SUBLANES), :], hist)


def _seq_chunk(proj, mxu_filler, x_ref,
               cw_ref, cb_ref, dtb_ref, alog_ref, dskip_ref, gn_ref,
               ccw_ref, ccb_ref, lng_ref, lnb_ref, wy_ref, wc_ref,
               h_ref, ssm_ref, shist_ref, chist_ref,
               st_ref, xbuf_ref, cbuf_ref, ctile_ref, *, d_ssm, n_heads, front_pad):
    c = pl.program_id(1)
    last = pl.num_programs(1) - 1
    is_meta = c == 0
    q = CHUNK
    n = SSD_STATE
    gw = SSD_GROUPS * n
    x_in = x_ref[0]

    xbc_raw = proj(1)
    xbuf_ref[SUBLANES:SUBLANES + q, :] = xbc_raw
    acc = cb_ref[...] + cw_ref[SSD_CONV - 1:SSD_CONV, :] * xbc_raw
    for j in range(1, SSD_CONV):
        mxu_filler()
        acc = acc + cw_ref[SSD_CONV - 1 - j:SSD_CONV - j, :] * xbuf_ref[SUBLANES - j:SUBLANES - j + q, :]
    xbuf_ref[0:SUBLANES, :] = xbuf_ref[q:q + SUBLANES, :]
    xact = _silu(acc)
    xs = xact[:, :d_ssm]
    bm = [xact[:, d_ssm + g * n:d_ssm + (g + 1) * n] for g in range(SSD_GROUPS)]
    cm = [xact[:, d_ssm + gw + g * n:d_ssm + gw + (g + 1) * n] for g in range(SSD_GROUPS)]

    dt_raw = proj(3)
    rows = lax.broadcasted_iota(jnp.int32, (q, LANES), 0)
    cols = lax.broadcasted_iota(jnp.int32, (q, LANES), 1)
    valid = jnp.logical_or(jnp.logical_not(is_meta), rows >= front_pad)
    dtv = jnp.where(valid, _softplus(dt_raw + dtb_ref[...]), 0.0)
    da = dtv * (-jnp.exp(alog_ref[...]))
    causal = rows >= cols
    tri = causal.astype(F32)
    a_cum = jnp.dot(tri, da, preferred_element_type=F32, precision=HIGHEST)
    a_cum_t = a_cum.T
    dt_t = dtv.T
    w_t = jnp.exp(a_cum_t[:, q - 1:q] - a_cum_t) * dt_t
    a_last = a_cum[q - 1:q, :]

    bm_t = [b.T for b in bm]
    cb = [lax.dot_general(cm[g].astype(BF16), bm[g].astype(BF16), (((1,), (1,)), ((), ())),
                          preferred_element_type=F32) for g in range(SSD_GROUPS)]
    lane = lax.broadcasted_iota(jnp.int32, (q, LANES), 1)
    lo = lane < SSD_HEAD_DIM
    heads_per_group = n_heads // SSD_GROUPS

    y_parts = []
    for j in range(n_heads // 2):
        g = (2 * j) // heads_per_group
        m_l, ce_l, bw_l = [], [], []
        for h in (2 * j, 2 * j + 1):
            col = a_cum[:, h:h + 1]
            seg = col - a_cum_t[h:h + 1, :]
            dec = jnp.exp(jnp.where(causal, seg, -jnp.inf))
            m_l.append((cb[g] * dec * dt_t[h:h + 1, :]).astype(BF16))
            ce_l.append((cm[g] * jnp.exp(col)).astype(BF16))
            bw_l.append((bm_t[g] * w_t[h:h + 1, :]).astype(BF16))
        sl = slice(j * LANES, (j + 1) * LANES)
        xs_pair = xs[:, sl]
        rhs_x = jnp.concatenate([jnp.where(lo, xs_pair, 0.0), jnp.where(lo, 0.0, xs_pair)],
                                axis=0).astype(BF16)
        st_pair = st_ref[:, sl]
        rhs_s = jnp.concatenate([jnp.where(lo, st_pair, 0.0), jnp.where(lo, 0.0, st_pair)],
                                axis=0).astype(BF16)
        lhs_y = jnp.concatenate(m_l + ce_l, axis=1)
        y_pair = jnp.dot(lhs_y, jnp.concatenate([rhs_x, rhs_s], axis=0), preferred_element_type=F32)
        st_new = jnp.dot(jnp.concatenate(bw_l, axis=1), rhs_x, preferred_element_type=F32)
        cd = jnp.exp(jnp.where(lo[0:1, :], a_last[:, 2 * j:2 * j + 1], a_last[:, 2 * j + 1:2 * j + 2]))
        st_ref[:, sl] = st_pair * cd + st_new
        y_parts.append(y_pair + dskip_ref[:, sl] * xs_pair)
        mxu_filler()

    y = jnp.concatenate(y_parts, axis=1)
    zf = proj(0)
    y = y * _silu(zf)
    gsz = d_ssm // SSD_GROUPS
    y = jnp.concatenate(
        [_rms(y[:, g * gsz:(g + 1) * gsz], gn_ref[:, g * gsz:(g + 1) * gsz]) for g in range(SSD_GROUPS)],
        axis=1)

    glu = proj(2)
    dc = glu.shape[1] // 2
    cval = glu[:, :dc] * _sigmoid(glu[:, dc:])
    hist = CONF_WIDTH - 1
    base = 32
    tile = lambda t: t * SUBLANES
    _rows_to_tiles(cbuf_ref.at[pl.ds(tile(base), tile(q)), :], cval)
    taps = ccw_ref[...].reshape(base, SUBLANES, LANES)
    tch = 16
    for t0 in range(0, q, tch):
        cacc = jnp.broadcast_to(ccb_ref[...][None], (tch, SUBLANES, LANES))
        for k in range(CONF_WIDTH):
            off = base - hist + k + t0
            window = cbuf_ref[pl.ds(tile(off), tile(tch)), :].reshape(tch, SUBLANES, LANES)
            cacc = cacc + taps[k][None] * window
        ctile_ref[pl.ds(tile(t0), tile(tch)), :] = cacc.reshape(tile(tch), LANES)
        mxu_filler()
    cbuf_ref[0:tile(base), :] = cbuf_ref[tile(q):tile(q + base), :]
    cacc = _tiles_to_rows(ctile_ref, q)
    mu = jnp.mean(cacc, axis=-1, keepdims=True)
    var = jnp.mean(jnp.square(cacc - mu), axis=-1, keepdims=True)
    cn = (cacc - mu) * lax.rsqrt(var + EPS) * lng_ref[...] + lnb_ref[...]
    cout = _silu(cn)

    h_ref[0] = (x_in + jnp.dot(y.astype(BF16), wy_ref[...], preferred_element_type=F32)
                + jnp.dot(cout.astype(BF16), wc_ref[...], preferred_element_type=F32))


def _seq(x, meta_proj, g_mix, w_in_parts, params, w_out_parts, *, n_heads):
    b, seq, d = x.shape
    xbcm, zm, glum, dtm = meta_proj
    wz, wx, wg, wd = w_in_parts
    conv_dim, d_ssm, dc = wx.shape[1], wz.shape[1], wg.shape[1] // 2
    assert dc == SUBLANES * LANES
    nc = seq // CHUNK + 1
    front_pad = (-N_META) % CHUNK
    blk = lambda n: pl.BlockSpec((1, CHUNK, n), lambda i, c: (i, jnp.maximum(c - 1, 0), 0))
    nxt = pl.BlockSpec((1, CHUNK, d), lambda i, c: (i, jnp.minimum(c, nc - 2), 0))
    full = lambda a: pl.BlockSpec(a.shape, lambda i, c: (0,) * a.ndim)
    consts = [zm, xbcm, glum, dtm, g_mix, wz, wx, wg, wd, *params, *w_out_parts]
    proj_bufs = [pltpu.VMEM((CHUNK, w.shape[1]), F32) for w in (wz, wx, wg, wd)]
    kern = functools.partial(_seq_kernel, d_ssm=d_ssm, n_heads=n_heads, front_pad=front_pad)
    return pl.pallas_call(
        kern,
        grid=(b, nc),
        in_specs=[blk(d), nxt] + [full(a) for a in consts],
        out_specs=[blk(d),
                   pl.BlockSpec((1, n_heads, SSD_HEAD_DIM, SSD_STATE), lambda i, c: (i, 0, 0, 0)),
                   pl.BlockSpec((1, SSD_CONV - 1, conv_dim), lambda i, c: (i, 0, 0)),
                   pl.BlockSpec((1, CONF_WIDTH - 1, dc), lambda i, c: (i, 0, 0))],
        out_shape=[jax.ShapeDtypeStruct((b, seq, d), F32),
                   jax.ShapeDtypeStruct((b, n_heads, SSD_HEAD_DIM, SSD_STATE), F32),
                   jax.ShapeDtypeStruct((b, SSD_CONV - 1, conv_dim), F32),
                   jax.ShapeDtypeStruct((b, CONF_WIDTH - 1, dc), F32)],
        scratch_shapes=[pltpu.VMEM((SSD_STATE, d_ssm), F32),
                        pltpu.VMEM((SUBLANES + CHUNK, conv_dim), F32),
                        pltpu.VMEM(((32 + CHUNK) * SUBLANES, LANES), F32),
                        pltpu.VMEM((CHUNK * SUBLANES, LANES), F32)] + proj_bufs + proj_bufs,
        compiler_params=_cparams(("arbitrary", "arbitrary")),
        name="seq",
    )(x, x, *consts)


def _sample_kernel(xbc_ref, z_ref, glu_ref, dt_ref, ssm_ref, shist_ref, chist_ref,
                   cw_ref, cb_ref, dtb_ref, alog_ref, dskip_ref, gn_ref,
                   ccw_ref, ccb_ref, lng_ref, lnb_ref,
                   y_ref, c_ref, ssm_o, shist_o, chist_o, *, d_ssm, n_heads):
    sb = xbc_ref.shape[0]
    n = SSD_STATE
    gw = SSD_GROUPS * n
    hp = d_ssm // SSD_GROUPS
    heads_per_group = n_heads // SSD_GROUPS

    x_new = xbc_ref[...].astype(F32)
    acc = cb_ref[...] + cw_ref[SSD_CONV - 1:SSD_CONV, :] * x_new
    for k in range(SSD_CONV - 1):
        acc = acc + cw_ref[k:k + 1, :] * shist_ref[:, k, :]
    for k in range(SSD_CONV - 2):
        shist_o[:, k, :] = shist_ref[:, k + 1, :]
    shist_o[:, SSD_CONV - 2, :] = x_new
    xact = _silu(acc)
    xs = xact[:, :d_ssm]
    dtv = _softplus(dt_ref[...] + dtb_ref[...])
    a = -jnp.exp(alog_ref[...])
    decay = jnp.exp(dtv * a)

    hsel = (lax.broadcasted_iota(jnp.int32, (LANES, d_ssm), 1) // SSD_HEAD_DIM
            == lax.broadcasted_iota(jnp.int32, (LANES, d_ssm), 0)).astype(F32)
    dt_x = jnp.dot(dtv, hsel, preferred_element_type=F32, precision=HIGHEST)
    xdt = (xs * dt_x).astype(BF16)
    rowid = lax.broadcasted_iota(jnp.int32, (sb, 1), 0)

    y_rows = []
    for i in range(sb):
        sel = rowid == i
        xi = jnp.where(sel, xdt, jnp.zeros_like(xdt))
        parts = []
        for g in range(SSD_GROUPS):
            bm = xact[:, d_ssm + g * n:d_ssm + (g + 1) * n].astype(BF16)
            cmat = xact[:, d_ssm + gw + g * n:d_ssm + gw + (g + 1) * n].astype(BF16)
            outer = lax.dot_general(xi[:, g * hp:(g + 1) * hp], bm, (((0,), (0,)), ((), ())),
                                    preferred_element_type=F32)
            outer = outer.reshape(heads_per_group, SSD_HEAD_DIM, n)
            hs = slice(g * heads_per_group, (g + 1) * heads_per_group)
            s_old = ssm_ref[i, hs]
            dec_i = jnp.stack([jnp.broadcast_to(decay[i:i + 1, h:h + 1], (SSD_HEAD_DIM, n))
                               for h in range(g * heads_per_group, (g + 1) * heads_per_group)])
            s_new = s_old * dec_i + outer
            ssm_o[i, hs] = s_new
            yv = lax.dot_general(cmat, s_new.reshape(hp, n).astype(BF16), (((1,), (1,)), ((), ())),
                                 preferred_element_type=F32)
            parts.append(yv)
        yfull = jnp.concatenate(parts, axis=1)
        y_rows.append(jnp.where(sel, yfull, 0.0))
    y = y_rows[0]
    for r in y_rows[1:]:
        y = y + r
    y = y + dskip_ref[...] * xs
    y = y * _silu(z_ref[...].astype(F32))
    gsz = d_ssm // SSD_GROUPS
    y = jnp.concatenate(
        [_rms(y[:, g * gsz:(g + 1) * gsz], gn_ref[:, g * gsz:(g + 1) * gsz]) for g in range(SSD_GROUPS)],
        axis=1)
    y_ref[...] = y.astype(BF16)

    glu = glu_ref[...].astype(F32)
    dc = glu.shape[1] // 2
    cval = glu[:, :dc] * _sigmoid(glu[:, dc:])
    hist = CONF_WIDTH - 1
    cacc = ccb_ref[...] + ccw_ref[hist:hist + 1, :] * cval
    cacc = cacc + jnp.sum(chist_ref[...] * ccw_ref[0:hist, :][None], axis=1)
    chist_o[:, 0:hist - 1, :] = chist_ref[:, 1:hist, :]
    chist_o[:, hist - 1, :] = cval
    mu = jnp.mean(cacc, axis=-1, keepdims=True)
    var = jnp.mean(jnp.square(cacc - mu), axis=-1, keepdims=True)
    cn = (cacc - mu) * lax.rsqrt(var + EPS) * lng_ref[...] + lnb_ref[...]
    c_ref[...] = _silu(cn).astype(BF16)


def _sample(xbc, z, glu, dt, ssm, shist, chist, params, *, n_heads):
    bs, conv_dim = xbc.shape
    d_ssm = z.shape[1]
    dc = glu.shape[1] // 2
    sb = SUBLANES
    assert bs % sb == 0
    row = lambda n: pl.BlockSpec((sb, n), lambda i: (i, 0))
    full = lambda a: pl.BlockSpec(a.shape, lambda i: (0,) * a.ndim)
    st_spec = pl.BlockSpec((sb, n_heads, SSD_HEAD_DIM, SSD_STATE), lambda i: (i, 0, 0, 0))
    sh_spec = pl.BlockSpec((sb, SSD_CONV - 1, conv_dim), lambda i: (i, 0, 0))
    ch_spec = pl.BlockSpec((sb, CONF_WIDTH - 1, dc), lambda i: (i, 0, 0))
    kern = functools.partial(_sample_kernel, d_ssm=d_ssm, n_heads=n_heads)
    return pl.pallas_call(
        kern,
        grid=(bs // sb,),
        in_specs=[row(conv_dim), row(d_ssm), row(2 * dc), row(LANES), st_spec, sh_spec, ch_spec]
        + [full(p) for p in params],
        out_specs=[row(d_ssm), row(dc), st_spec, sh_spec, ch_spec],
        out_shape=[jax.ShapeDtypeStruct((bs, d_ssm), BF16),
                   jax.ShapeDtypeStruct((bs, dc), BF16),
                   jax.ShapeDtypeStruct(ssm.shape, F32),
                   jax.ShapeDtypeStruct(shist.shape, F32),
                   jax.ShapeDtypeStruct(chist.shape, F32)],
        compiler_params=_cparams(("parallel",)),
        name="sample_step",
    )(xbc, z, glu, dt, ssm, shist, chist, *params)


def _router_kernel(*refs, with_proj):
    if with_proj:
        y_ref, c_ref, res_ref, wy_ref, wc_ref = refs[:5]
        g_ref, wr_ref, br_ref, cnt_in_ref, _, h_ref = refs[5:11]
        xn_ref, idx_ref, gate_ref, rank_ref, cnt_ref, carry_ref = refs[11:]
        h = (res_ref[...]
             + jnp.dot(y_ref[...], wy_ref[...], preferred_element_type=F32)
             + jnp.dot(c_ref[...], wc_ref[...], preferred_element_type=F32))
        h_ref[...] = h
    else:
        h_in_ref, g_ref, wr_ref, br_ref, cnt_in_ref, _ = refs[:6]
        xn_ref, idx_ref, gate_ref, rank_ref, cnt_ref, carry_ref = refs[6:]
        h = h_in_ref[...]
    i = pl.program_id(0)

    @pl.when(i == 0)
    def _():
        carry_ref[...] = cnt_in_ref[...]

    xn = _rms(h, g_ref[...])
    _rows_to_tiles(xn_ref, xn)
    tm = h.shape[0]
    logits = lax.dot_general(wr_ref[...], xn, (((1,), (1,)), ((), ())),
                             preferred_element_type=F32, precision=HIGHEST) + br_ref[...]
    eid = lax.broadcasted_iota(jnp.int32, logits.shape, 0)
    vals, idxs, hots = [], [], []
    work = logits
    for _ in range(TOP_K):
        m = jnp.max(work, axis=0, keepdims=True)
        sel = jnp.min(jnp.where(work == m, eid, N_EXPERTS), axis=0, keepdims=True)
        hot = eid == sel
        work = jnp.where(hot, -jnp.inf, work)
        vals.append(m)
        idxs.append(sel)
        hots.append(hot)
    exps = [jnp.exp(v - vals[0]) for v in vals]
    den = exps[0]
    for e in exps[1:]:
        den = den + e
    chosen = hots[0]
    for hot in hots[1:]:
        chosen = jnp.logical_or(chosen, hot)
    chosen_f = chosen.astype(F32)
    r = lax.broadcasted_iota(jnp.int32, (tm, tm), 0)
    cidx = lax.broadcasted_iota(jnp.int32, (tm, tm), 1)
    upper = (r < cidx).astype(BF16)
    cum = jnp.dot(chosen_f.astype(BF16), upper, preferred_element_type=F32) + carry_ref[:, 0:1]
    for k in range(TOP_K):
        idx_ref[k:k + 1, :] = idxs[k]
        gate_ref[k:k + 1, :] = exps[k] / den
        rank_ref[k:k + 1, :] = jnp.sum(jnp.where(hots[k], cum, 0.0), axis=0, keepdims=True).astype(jnp.int32)
    carry_ref[...] = carry_ref[...] + jnp.sum(chosen_f, axis=1, keepdims=True)
    cnt_ref[...] = carry_ref[...]


def _router(h_or_proj, g, wr_t, br, cnt_in, xn_buf, row_off):
    with_proj = isinstance(h_or_proj, tuple)
    t, d = (h_or_proj[2] if with_proj else h_or_proj).shape
    assert d == SUBLANES * LANES
    tm = min(512, t)
    assert t % tm == 0 and row_off % tm == 0
    off = row_off // tm
    row = lambda n: pl.BlockSpec((tm, n), lambda i: (i, 0))
    col = pl.BlockSpec((TOP_K, tm), lambda i: (0, i))
    full = lambda a: pl.BlockSpec(a.shape, lambda i: (0,) * a.ndim)
    if with_proj:
        y, c, res, wy, wc = h_or_proj
        args = [y, c, res, wy, wc]
        in_specs = [row(y.shape[1]), row(c.shape[1]), row(d), full(wy), full(wc)]
        out_specs, out_shape = [row(d)], [jax.ShapeDtypeStruct((t, d), F32)]
    else:
        args, in_specs, out_specs, out_shape = [h_or_proj], [row(d)], [], []
    args += [g, wr_t, br, cnt_in, xn_buf]
    in_specs += [full(g), full(wr_t), full(br), full(cnt_in), pl.BlockSpec(memory_space=pl.ANY)]
    out_specs += [pl.BlockSpec((tm * SUBLANES, LANES), lambda i: (i + off, 0)), col, col, col, full(cnt_in)]
    out_shape += [jax.ShapeDtypeStruct(xn_buf.shape, F32),
                  jax.ShapeDtypeStruct((TOP_K, t), jnp.int32),
                  jax.ShapeDtypeStruct((TOP_K, t), F32),
                  jax.ShapeDtypeStruct((TOP_K, t), jnp.int32),
                  jax.ShapeDtypeStruct(cnt_in.shape, F32)]
    return pl.pallas_call(
        functools.partial(_router_kernel, with_proj=with_proj),
        grid=(t // tm,),
        in_specs=in_specs,
        out_specs=out_specs,
        out_shape=out_shape,
        scratch_shapes=[pltpu.VMEM(cnt_in.shape, F32)],
        input_output_aliases={len(args) - 1: len(out_shape) - 5},
        compiler_params=_cparams(("arbitrary",)),
        name="router",
    )(*args)


_INV_GROUP = 16


def _inverse_kernel(pad_lo_ref, pad_hi_ref, dest_ref, inv_ref, *, pad_base, fill_groups):
    step = pl.program_id(0)

    @pl.when(step == 0)
    def _():
        def sentinel(d, carry):
            inv_ref[d] = (pad_base + jnp.bitwise_and(d, MOE_BLOCK - 1)) * TOP_K
            return carry

        for e in range(pad_lo_ref.shape[0]):
            lax.fori_loop(pad_lo_ref[e], pad_hi_ref[e], sentinel, 0)

    @pl.when(step > 0)
    def _():
        def fill(jj, carry):
            a0 = ((step - 1) * fill_groups + jj) * _INV_GROUP
            for c in range(_INV_GROUP):
                inv_ref[dest_ref[a0 + c]] = a0 + c
            return carry

        lax.fori_loop(0, fill_groups, fill, 0)


def _build_inverse(dest, pad_lo, pad_hi, n_rows, pad_base):
    n_fill = dest.shape[0] // _INV_GROUP
    assert dest.shape[0] % _INV_GROUP == 0
    slices = max(g for g in range(1, 17) if n_fill % g == 0)
    kern = functools.partial(_inverse_kernel, pad_base=pad_base, fill_groups=n_fill // slices)
    smem = pl.BlockSpec(memory_space=pltpu.SMEM)
    return pl.pallas_call(
        kern,
        grid=(1 + slices,),
        in_specs=[smem, smem, smem],
        out_specs=smem,
        out_shape=jax.ShapeDtypeStruct((n_rows,), jnp.int32),
        compiler_params=_cparams(("arbitrary",)),
        name="inverse_table",
    )(pad_lo, pad_hi, dest)


def _moe_kernel(be_ref, na_ref, tok_ref, row_ref, eseq_ref, enext_ref,
                xn_hbm, wg_hbm, bg_ref, wu_hbm, bu_ref, wd_hbm, bd_ref,
                yk_hbm, wg_s, wu_s, wd_s, x0, x1, x2, x3, x4, x5, o0, o1, gsem, ssem,
                stg_g, stg_u, stg_d, wsem, *, trash_row):
    i = pl.program_id(0)
    n_active = na_ref[0]
    last_block = pl.num_programs(0) - 2
    xbuf, obuf = (x0, x1, x2, x3, x4, x5), (o0, o1)
    depth = len(xbuf)
    ahead = depth // 2
    prev = be_ref[jnp.maximum(i - 1, 0)]
    new_expert = jnp.logical_or(i == 0, be_ref[i] != prev)
    active = i < n_active

    def gather_start(block, slot):
        for r in range(MOE_BLOCK):
            src = pl.multiple_of(tok_ref[block * (MOE_BLOCK // LANES) + r // LANES, r % LANES], SUBLANES)
            pltpu.make_async_copy(xn_hbm.at[pl.ds(src, SUBLANES), :],
                                  xbuf[slot].at[pl.ds(r * SUBLANES, SUBLANES), :],
                                  gsem.at[slot]).start(priority=0)

    def gather_wait(slot):
        pltpu.make_async_copy(xn_hbm.at[pl.ds(0, MOE_BLOCK * SUBLANES), :], xbuf[slot],
                              gsem.at[slot]).wait()

    def scatter_start(block, slot, real, trash):
        for r in range(MOE_BLOCK):
            dst = jnp.where(real, row_ref[block * (MOE_BLOCK // LANES) + r // LANES, r % LANES],
                            (trash + r) * SUBLANES)
            pltpu.make_async_copy(obuf[slot].at[pl.ds(r * SUBLANES, SUBLANES), :],
                                  yk_hbm.at[pl.ds(pl.multiple_of(dst, SUBLANES), SUBLANES), :],
                                  ssem.at[slot]).start(priority=1)

    def scatter_wait(slot):
        pltpu.make_async_copy(obuf[slot], yk_hbm.at[pl.ds(0, MOE_BLOCK * SUBLANES), :],
                              ssem.at[slot]).wait()

    @pl.when(i == 0)
    def _():
        o0[...] = jnp.zeros_like(o0)
        o1[...] = jnp.zeros_like(o1)
        for a in range(ahead):
            gather_start(jnp.minimum(a, last_block), a)
        scatter_start(0, 0, False, trash_row)

    def weight_copies(expert, slot):
        return [pltpu.make_async_copy(w_hbm.at[expert], stg.at[slot], wsem.at[slot])
                for w_hbm, stg in ((wg_hbm, stg_g), (wu_hbm, stg_u), (wd_hbm, stg_d))]

    @pl.when(i == 0)
    def _():
        for cp in weight_copies(be_ref[0], 0):
            cp.start()

    @pl.when(jnp.logical_and(active, new_expert))
    def _():
        slot = eseq_ref[i] % 2
        for cp in weight_copies(be_ref[i], slot):
            cp.wait()
        wg_s[...] = stg_g[slot].astype(BF16)
        wu_s[...] = stg_u[slot].astype(BF16)
        wd_s[...] = stg_d[slot].astype(BF16)
        upcoming = enext_ref[i]

        @pl.when(upcoming >= 0)
        def _():
            for cp in weight_copies(upcoming, 1 - slot):
                cp.start(priority=1)

    for q in range(depth):
        mine = (i % depth) == q
        p = q % 2

        @pl.when(jnp.logical_and(active, mine))
        def _():
            gather_wait(q)
            gather_start(jnp.minimum(i + ahead, last_block), (q + ahead) % depth)
            x = _tiles_to_rows(xbuf[q], MOE_BLOCK).astype(BF16)
            gt = jnp.minimum(jnp.dot(x, wg_s[...], preferred_element_type=F32) + bg_ref[0], SWIGLU_LIMIT)
            up = jnp.clip(jnp.dot(x, wu_s[...], preferred_element_type=F32) + bu_ref[0],
                          -SWIGLU_LIMIT, SWIGLU_LIMIT)
            hdn = (up + 1.0) * gt * _sigmoid(gt * SWIGLU_ALPHA)
            scatter_start(jnp.maximum(i - 1, 0), 1 - p, i > 0, trash_row + MOE_BLOCK)
            y = jnp.dot(hdn.astype(BF16), wd_s[...], preferred_element_type=F32) + bd_ref[0]
            scatter_wait(p)
            _rows_to_tiles(obuf[p], y)

        @pl.when(jnp.logical_and(i == n_active, mine))
        def _():
            for a in range(ahead):
                gather_wait((q + a) % depth)
            scatter_wait(p)
            scatter_start(i - 1, 1 - p, True, trash_row)
            scatter_wait(1 - p)


def _moe(block_expert, n_active, tok2d, row2d, expert_seq, expert_next, xn,
         w_gate, b_gate, w_up, b_up, w_down, b_down, *, yk_rows, trash_row):
    d = w_gate.shape[1]
    assert d == SUBLANES * LANES and xn.shape[1] == LANES
    nb = tok2d.shape[0] * LANES // MOE_BLOCK
    dff = w_gate.shape[2]
    bspec = lambda a: pl.BlockSpec((1,) + a.shape[1:], lambda i, be, *_: (be[i], 0, 0))
    hbm = pl.BlockSpec(memory_space=pl.ANY)
    blk = pltpu.VMEM((MOE_BLOCK * SUBLANES, LANES), F32)
    kern = functools.partial(_moe_kernel, trash_row=trash_row)
    return pl.pallas_call(
        kern,
        grid_spec=pltpu.PrefetchScalarGridSpec(
            num_scalar_prefetch=6,
            grid=(nb + 1,),
            in_specs=[hbm, hbm, bspec(b_gate), hbm, bspec(b_up), hbm, bspec(b_down)],
            out_specs=hbm,
            scratch_shapes=[pltpu.VMEM((d, dff), BF16), pltpu.VMEM((d, dff), BF16),
                            pltpu.VMEM((dff, d), BF16), blk, blk, blk, blk, blk, blk, blk, blk,
                            pltpu.SemaphoreType.DMA((6,)), pltpu.SemaphoreType.DMA((2,)),
                            pltpu.VMEM((2, d, dff), F32), pltpu.VMEM((2, d, dff), F32),
                            pltpu.VMEM((2, dff, d), F32), pltpu.SemaphoreType.DMA((2,))],
        ),
        out_shape=jax.ShapeDtypeStruct((yk_rows * SUBLANES, LANES), F32),
        compiler_params=_cparams(("arbitrary",)),
        name="moe_experts",
    )(block_expert, n_active, tok2d, row2d, expert_seq, expert_next, xn,
      w_gate, b_gate, w_up, b_up, w_down, b_down)


def _combine_kernel(h_ref, gate_ref, gf_ref, *rest):
    yk_refs, o_ref = rest[:TOP_K], rest[TOP_K]
    acc = h_ref[...]
    tm = acc.shape[0]
    for k in range(TOP_K):
        acc = acc + gate_ref[:, k:k + 1] * _tiles_to_rows(yk_refs[k], tm)
    o_ref[...] = _rms(acc, gf_ref[...])


def _combine(h, gates, g_final, yk, row_off, plane_tokens):
    t, d = h.shape
    tm = min(512, t)
    assert t % tm == 0
    plane = lambda k: pl.BlockSpec(
        (pl.Element(tm * SUBLANES), pl.Element(LANES)),
        lambda i: ((k * plane_tokens + row_off + i * tm) * SUBLANES, 0))
    return pl.pallas_call(
        _combine_kernel,
        grid=(t // tm,),
        in_specs=[pl.BlockSpec((tm, d), lambda i: (i, 0)),
                  pl.BlockSpec((tm, TOP_K), lambda i: (i, 0)),
                  pl.BlockSpec(g_final.shape, lambda i: (0, 0))] + [plane(k) for k in range(TOP_K)],
        out_specs=pl.BlockSpec((tm, d), lambda i: (i, 0)),
        out_shape=jax.ShapeDtypeStruct((t, d), F32),
        compiler_params=_cparams(("parallel",)),
        name="combine",
    )(h, gates, g_final, *([yk] * TOP_K))


def _pad_lanes(v, fill=0.0):
    v = v.reshape(1, -1).astype(F32)
    return jnp.pad(v, ((0, 0), (0, LANES - v.shape[1])), constant_values=fill)


def kernel(x_prompt, x_sample, state_ssm, state_ssd_conv, state_conf_conv, meta_tokens, g_mix, w_in,
           conv_ssd_w, conv_ssd_b, dt_bias, a_log, d_skip, g_ssd_norm, conv_conf_w, conv_conf_b,
           ln_conf_g, ln_conf_b, w_out, g_ffn, w_router, b_router, w_gate, b_gate, w_up, b_up,
           w_down, b_down, g_final):
    bp, seq, d = x_prompt.shape
    bs = x_sample.shape[0]
    depth = w_in.shape[0]
    assert depth == 1 and x_sample.shape[1] == 1 and seq % CHUNK == 0
    n_heads = a_log.shape[1]
    d_ssm = n_heads * SSD_HEAD_DIM
    conv_dim = conv_ssd_w.shape[2]
    dc = conv_conf_w.shape[2]
    l = 0

    o1, o2, o3 = d_ssm, d_ssm + conv_dim, d_ssm + conv_dim + n_heads
    w_in_l = w_in[l]
    wz = w_in_l[:, :o1].astype(BF16)
    wx = w_in_l[:, o1:o2].astype(BF16)
    wdt = jnp.pad(w_in_l[:, o2:o3], ((0, 0), (0, LANES - n_heads))).astype(BF16)
    wglu = w_in_l[:, o3:].astype(BF16)
    row2 = lambda v: v.reshape(1, -1).astype(F32)
    seq_params = (
        conv_ssd_w[l], row2(conv_ssd_b[l]), _pad_lanes(dt_bias[l]), _pad_lanes(a_log[l]),
        row2(jnp.repeat(d_skip[l], SSD_HEAD_DIM)), row2(g_ssd_norm[l]),
        jnp.pad(conv_conf_w[l], ((0, 32 - CONF_WIDTH), (0, 0))), row2(conv_conf_b[l]),
        row2(ln_conf_g[l]), row2(ln_conf_b[l]))
    g_mix_r = row2(g_mix[l])
    w_out_l = w_out[l]
    wy = w_out_l[:d_ssm].astype(BF16)
    wc = w_out_l[d_ssm:].astype(BF16)
    prompt_params = seq_params[:6] + (
        seq_params[6].reshape(32 * SUBLANES, LANES), conv_conf_b[l].reshape(SUBLANES, LANES).astype(F32)
    ) + seq_params[8:]

    front_pad = (-N_META) % CHUNK
    small = jnp.concatenate([jnp.zeros((front_pad, d), F32), meta_tokens.astype(F32),
                             x_sample.reshape(bs, d)], axis=0)
    z_m, xbc_m, glu_m, dt_m = _inproj(small, g_mix_r, wz, wx, wglu, wdt)

    h_p, ssm_p, shist_p, chist_p = _seq(
        x_prompt, (xbc_m[:CHUNK], z_m[:CHUNK], glu_m[:CHUNK], dt_m[:CHUNK]), g_mix_r,
        (wz, wx, wglu, wdt), prompt_params, (wy, wc), n_heads=n_heads)

    y_s, c_s, ssm_s, shist_s, chist_s = _sample(
        xbc_m[CHUNK:], z_m[CHUNK:], glu_m[CHUNK:], dt_m[CHUNK:],
        state_ssm[l], state_ssd_conv[l], state_conf_conv[l], seq_params, n_heads=n_heads)

    g_ffn_r = row2(g_ffn[l])
    wr_t = w_router[l].T.astype(F32)
    br = b_router[l].reshape(N_EXPERTS, 1).astype(F32)
    cnt0 = jnp.zeros((N_EXPERTS, LANES), F32)
    t_p = bp * seq
    t_all = t_p + bs
    assert (t_all * TOP_K) % LANES == 0
    h_p = h_p.reshape(t_p, d)
    xn_all = jnp.zeros((t_all * SUBLANES, LANES), F32)
    xn_all, idx_p, gate_p, rank_p, cnt1 = _router(h_p, g_ffn_r, wr_t, br, cnt0, xn_all, 0)
    h_s, xn_all, idx_s, gate_s, rank_s, cnt2 = _router(
        (y_s, c_s, x_sample.reshape(bs, d), wy, wc), g_ffn_r, wr_t, br, cnt1, xn_all, t_p)

    counts = cnt2[:, 0].astype(jnp.int32)
    padded = (counts + MOE_BLOCK - 1) // MOE_BLOCK * MOE_BLOCK
    pad_end = jnp.cumsum(padded)
    pad_start = pad_end - padded
    n_blocks = -(-(t_all * TOP_K) // MOE_BLOCK) + N_EXPERTS
    rows = n_blocks * MOE_BLOCK
    blk_first = jnp.arange(n_blocks + 1, dtype=jnp.int32) * MOE_BLOCK
    block_expert = jnp.minimum(jnp.sum(pad_end[None, :] <= blk_first[:, None], axis=1),
                               N_EXPERTS - 1).astype(jnp.int32)
    n_active = (pad_end[-1] // MOE_BLOCK).astype(jnp.int32).reshape(1)
    idx_all = jnp.concatenate([idx_p, idx_s], axis=1)
    rank_all = jnp.concatenate([rank_p, rank_s], axis=1)
    experts = jnp.arange(N_EXPERTS, dtype=jnp.int32)[:, None, None]
    start_of = jnp.sum(jnp.where(idx_all[None] == experts, pad_start[:, None, None], 0), axis=0)
    dest = (start_of + rank_all).T.reshape(-1).astype(jnp.int32)

    trash = TOP_K * t_all
    pad_lo = jnp.concatenate([pad_start + counts, pad_end[-1:]]).astype(jnp.int32)
    pad_hi = jnp.concatenate([pad_end, jnp.full((1,), rows, jnp.int32)]).astype(jnp.int32)
    inv = _build_inverse(dest, pad_lo, pad_hi, rows, t_all).reshape(rows // LANES, LANES)
    inv_tok = inv // TOP_K
    tok2d = jnp.minimum(inv_tok, t_all - 1) * SUBLANES
    row2d = jnp.where(inv_tok < t_all, (inv % TOP_K) * t_all + inv_tok, trash + inv_tok - t_all) * SUBLANES
    owns = counts > 0
    ids = jnp.arange(N_EXPERTS, dtype=jnp.int32)
    position = (jnp.cumsum(owns) - owns).astype(jnp.int32)
    later = jnp.where(owns[None, :] & (ids[None, :] > ids[:, None]), ids[None, :], N_EXPERTS)
    following = jnp.min(later, axis=1)
    following = jnp.where(following < N_EXPERTS, following, -1).astype(jnp.int32)
    hot = block_expert[:, None] == ids[None, :]
    expert_seq = jnp.sum(jnp.where(hot, position[None, :], 0), axis=1).astype(jnp.int32)
    expert_next = jnp.sum(jnp.where(hot, following[None, :], 0), axis=1).astype(jnp.int32)
    yk = _moe(block_expert, n_active, tok2d, row2d, expert_seq, expert_next, xn_all,
              w_gate[l], b_gate[l].reshape(N_EXPERTS, 1, -1), w_up[l], b_up[l].reshape(N_EXPERTS, 1, -1),
              w_down[l], b_down[l].reshape(N_EXPERTS, 1, -1), yk_rows=trash + 2 * MOE_BLOCK, trash_row=trash)
    g_fin = row2(g_final)
    y_prompt = _combine(h_p, gate_p.T, g_fin, yk, 0, t_all).reshape(bp, seq, d)
    y_sample = _combine(h_s, gate_s.T, g_fin, yk, t_p, t_all).reshape(bs, 1, d)

    return (y_prompt, y_sample, ssm_p[None], shist_p[None], chist_p[None],
            ssm_s[None], shist_s[None], chist_s[None])
```
